```python
import jax, jax.numpy as jnp
from jax import lax
import numpy as np

D_MODEL = 2048
BATCH = 32
SEQ = 256
DEPTH = 2
DEC_BATCH = 4
DEC_SEQ = 2048
PAST_LEN = 256

GRID_W = 64
HEAD_DIM = 128
ATTN_WIDTH = D_MODEL // 2
ATTN_HEADS = ATTN_WIDTH // HEAD_DIM
ATTN_KV_HEADS = 2
ATTN_GROUP = ATTN_HEADS // ATTN_KV_HEADS
KV_WIDTH = ATTN_KV_HEADS * HEAD_DIM
WINDOW = 128
BLOCK = 128
GLA_WIDTH = D_MODEL - ATTN_WIDTH
GLA_HEADS = 4
GLA_DV = GLA_WIDTH // GLA_HEADS
GLA_DK = GLA_DV // 2
GLA_KEY_WIDTH = GLA_HEADS * GLA_DK
GATE_RANK = 16
GATE_NORM = 16.0
GLA_CHUNK = 64
IN_SPLIT = (ATTN_WIDTH, KV_WIDTH, KV_WIDTH, GLA_KEY_WIDTH, GLA_KEY_WIDTH, GLA_WIDTH, GLA_WIDTH, GATE_RANK, GATE_RANK)
IN_WIDTH = ATTN_WIDTH + 2 * KV_WIDTH + 2 * GLA_KEY_WIDTH + 2 * GLA_WIDTH + 2 * GATE_RANK
D_FF = 5632
CONV_WIDTH = 3
ROPE_THETA = 10000.0
LN_EPS = 1e-5
NEG_INF = -1e30
DEEPNORM_ALPHA = (2 * DEPTH) ** 0.25
DEEPNORM_BETA = (8 * DEPTH) ** -0.25

kernel_name = 'hybrid_swa_gla_deepnorm_diffusion_step'


def _split_cols(a, sizes):
    offs, s = [], 0
    for n in sizes[:-1]:
        s += n
        offs.append(s)
    return jnp.split(a, offs, axis=-1)


def _layer_norm(x, w, b):
    xf = x.astype(jnp.float32)
    mu = jnp.mean(xf, axis=-1, keepdims=True)
    var = jnp.mean(jnp.square(xf - mu), axis=-1, keepdims=True)
    y = (xf - mu) * lax.rsqrt(var + LN_EPS) * w.astype(jnp.float32) + b.astype(jnp.float32)
    return y.astype(x.dtype)


def _rope_2d(x):
    T = x.shape[1]
    rows = T // GRID_W
    row = jnp.repeat(jnp.arange(rows), GRID_W).astype(jnp.float32)
    col = jnp.tile(jnp.arange(GRID_W), rows).astype(jnp.float32)
    half = HEAD_DIM // 2
    n_freq = half // 2
    freqs = ROPE_THETA ** (-jnp.arange(n_freq, dtype=jnp.float32) / n_freq)

    def rot(xa, pos):
        ang = pos[:, None] * freqs[None, :]
        cos = jnp.cos(ang)[None, :, None, :]
        sin = jnp.sin(ang)[None, :, None, :]
        x1, x2 = xa[..., :n_freq], xa[..., n_freq:]
        return jnp.concatenate([x1 * cos - x2 * sin, x1 * sin + x2 * cos], axis=-1)

    xf = x.astype(jnp.float32)
    return jnp.concatenate([rot(xf[..., :half], row), rot(xf[..., half:], col)], axis=-1).astype(x.dtype)


def _sink_softmax(sink, s):
    sk = sink.astype(jnp.float32).reshape(1, ATTN_KV_HEADS, ATTN_GROUP, 1, 1)
    sk = jnp.broadcast_to(sk, s.shape[:-1] + (1,))
    p = jax.nn.softmax(jnp.concatenate([sk, s], axis=-1), axis=-1)
    return p[..., 1:]


def _context_attention(q, k, v, sink):
    B, S = q.shape[:2]
    nb = S // BLOCK
    scale = HEAD_DIM ** -0.5
    qb = q.reshape(B, nb, BLOCK, ATTN_KV_HEADS, ATTN_GROUP, HEAD_DIM).transpose(1, 0, 2, 3, 4, 5)

    def one(qi):
        s = jnp.einsum('bqkgd,bskd->bkgqs', qi, k).astype(jnp.float32) * scale
        p = _sink_softmax(sink, s).astype(v.dtype)
        return jnp.einsum('bkgqs,bskd->bqkgd', p, v)

    o = lax.map(one, qb)
    return o.transpose(1, 0, 2, 3, 4, 5).reshape(B, S, ATTN_WIDTH)


def _latent_attention(q, k, v, ck, cv, sink):
    B, T = q.shape[:2]
    nb = T // BLOCK
    scale = HEAD_DIM ** -0.5
    qb = q.reshape(B, nb, BLOCK, ATTN_KV_HEADS, ATTN_GROUP, HEAD_DIM).transpose(1, 0, 2, 3, 4, 5)
    pad = ((0, 0), (BLOCK, BLOCK), (0, 0), (0, 0))
    kp = jnp.pad(k, pad)
    vp = jnp.pad(v, pad)
    key_off = jnp.arange(3 * BLOCK) - BLOCK
    q_off = jnp.arange(BLOCK)

    def one(args):
        qi, i = args
        start = i * BLOCK
        kb = lax.dynamic_slice_in_dim(kp, start, 3 * BLOCK, axis=1)
        vb = lax.dynamic_slice_in_dim(vp, start, 3 * BLOCK, axis=1)
        kpos = start + key_off
        qpos = start + q_off
        valid = (jnp.abs(qpos[:, None] - kpos[None, :]) <= WINDOW) & (kpos >= 0)[None, :] & (kpos < T)[None, :]
        s_loc = jnp.einsum('bqkgd,bskd->bkgqs', qi, kb).astype(jnp.float32) * scale
        s_loc = jnp.where(valid, s_loc, NEG_INF)
        s_ctx = jnp.einsum('bqkgd,bskd->bkgqs', qi, ck).astype(jnp.float32) * scale
        p = _sink_softmax(sink, jnp.concatenate([s_loc, s_ctx], axis=-1)).astype(v.dtype)
        return (jnp.einsum('bkgqs,bskd->bqkgd', p[..., :3 * BLOCK], vb)
                + jnp.einsum('bkgqs,bskd->bqkgd', p[..., 3 * BLOCK:], cv))

    o = lax.map(one, (qb, jnp.arange(nb)))
    return o.transpose(1, 0, 2, 3, 4, 5).reshape(B, T, ATTN_WIDTH)


def _gla_direction(q, k, v, g, s0):
    B, T, H, DK = q.shape
    DV = v.shape[-1]
    N = T // GLA_CHUNK
    q = q.reshape(B, N, GLA_CHUNK, H, DK)
    k = k.reshape(B, N, GLA_CHUNK, H, DK)
    v = v.reshape(B, N, GLA_CHUNK, H, DV)
    g = g.reshape(B, N, GLA_CHUNK, H, DK)
    b = jnp.cumsum(g, axis=2)
    b_last = b[:, :, -1:]
    qe = q * jnp.exp(b)
    ke = k * jnp.exp(-b)
    kd = k * jnp.exp(b_last - b)
    causal = jnp.tril(jnp.ones((GLA_CHUNK, GLA_CHUNK), dtype=bool))
    a = jnp.einsum('bnihd,bnjhd->bnhij', qe, ke)
    a = jnp.where(causal, a, 0.0)
    o = jnp.einsum('bnhij,bnjhv->bnihv', a, v)
    u = jnp.einsum('bnjhd,bnjhv->bnhdv', kd, v)
    decay = jnp.exp(b[:, :, -1])

    def step(s, inp):
        dcy, ui = inp
        return dcy[..., None] * s + ui, s

    s_fin, s_in = lax.scan(step, s0, (decay.transpose(1, 0, 2, 3), u.transpose(1, 0, 2, 3, 4)))
    s_in = s_in.transpose(1, 0, 2, 3, 4)
    o = o + jnp.einsum('bnihd,bnhdv->bnihv', qe, s_in)
    return o.reshape(B, T, H, DV), s_fin


def _gla(gq, gk, gv, gog, lr_f, lr_b, p, s0f, s0b):
    B, T = gq.shape[:2]
    f32 = jnp.float32
    q = gq.reshape(B, T, GLA_HEADS, GLA_DK).astype(f32) * (GLA_DK ** -0.5)
    k = gk.reshape(B, T, GLA_HEADS, GLA_DK).astype(f32)
    v = gv.reshape(B, T, GLA_HEADS, GLA_DV).astype(f32)
    gf = jax.nn.log_sigmoid((lr_f @ p['w_gate_f'] + p['b_gate_f']).astype(f32)) / GATE_NORM
    gb = jax.nn.log_sigmoid((lr_b @ p['w_gate_b'] + p['b_gate_b']).astype(f32)) / GATE_NORM
    gf = gf.reshape(B, T, GLA_HEADS, GLA_DK)
    gb = gb.reshape(B, T, GLA_HEADS, GLA_DK)
    of, sf = _gla_direction(q, k, v, gf, s0f.astype(f32))
    ob, sb = _gla_direction(q[:, ::-1], k[:, ::-1], v[:, ::-1], gb[:, ::-1], s0b.astype(f32))
    o = of + ob[:, ::-1]
    o = o * lax.rsqrt(jnp.mean(jnp.square(o), axis=-1, keepdims=True) + LN_EPS) * p['gla_norm_w'].astype(f32)
    o = o.reshape(B, T, GLA_WIDTH) * jax.nn.silu(gog.astype(f32))
    return o.astype(gq.dtype), sf.astype(gq.dtype), sb.astype(gq.dtype)


def _mixer(h, p, s0f, s0b, ctx_kv):
    B, T = h.shape[:2]
    q, k, v, gq, gk, gv, gog, lr_f, lr_b = _split_cols(h @ p['w_in'], IN_SPLIT)
    q = q.reshape(B, T, ATTN_HEADS, HEAD_DIM)
    k = k.reshape(B, T, ATTN_KV_HEADS, HEAD_DIM)
    v = v.reshape(B, T, ATTN_KV_HEADS, HEAD_DIM)
    if ctx_kv is None:
        o_att = _context_attention(q, k, v, p['attn_sink'])
    else:
        q = _rope_2d(q)
        k = _rope_2d(k)
        o_att = _latent_attention(q, k, v, ctx_kv[0], ctx_kv[1], p['attn_sink'])
    o_gla, sf, sb = _gla(gq, gk, gv, gog, lr_f, lr_b, p, s0f, s0b)
    out = jnp.concatenate([o_att, o_gla], axis=-1) @ p['w_out']
    return out, k, v, sf, sb


def _conv_ffn(h, p):
    u = h @ p['w_up']
    up = jnp.pad(u, ((0, 0), (1, 1), (0, 0)))
    w = p['conv_w']
    u = up[:, :-2] * w[0] + up[:, 1:-1] * w[1] + up[:, 2:] * w[2] + p['conv_b']
    a, g = jnp.split(u, 2, axis=-1)
    return (jax.nn.silu(g) * a) @ p['w_down']


def _layer(x, mods, p, s0f, s0b, ctx_kv):
    sh1, sc1, g1, sh2, sc2, g2 = jnp.split(mods, 6, axis=-1)
    out, k, v, sf, sb = _mixer(x * (1 + sc1) + sh1, p, s0f, s0b, ctx_kv)
    x = _layer_norm(DEEPNORM_ALPHA * x + g1 * out, p['ln1_w'], p['ln1_b'])
    f = _conv_ffn(x * (1 + sc2) + sh2, p)
    x = _layer_norm(DEEPNORM_ALPHA * x + g2 * f, p['ln2_w'], p['ln2_b'])
    return x, k, v, sf, sb


def setup_inputs(seed: int = 0) -> dict:
    key = jax.random.key(seed)
    ks = jax.random.split(key, 32)
    f32 = jnp.float32

    def nrm(k, shape, scale=1.0):
        return jax.random.normal(k, shape, f32) * scale

    return {
        'x_prompt': nrm(ks[0], (BATCH, SEQ, D_MODEL)),
        'x_sample': nrm(ks[1], (DEC_BATCH, DEC_SEQ, D_MODEL)),
        'cache_k': nrm(ks[2], (DEC_BATCH, DEPTH, PAST_LEN, ATTN_KV_HEADS, HEAD_DIM)),
        'cache_v': nrm(ks[3], (DEC_BATCH, DEPTH, PAST_LEN, ATTN_KV_HEADS, HEAD_DIM)),
        'state_gla_fwd': nrm(ks[4], (DEC_BATCH, DEPTH, GLA_HEADS, GLA_DK, GLA_DV)),
        'state_gla_bwd': nrm(ks[5], (DEC_BATCH, DEPTH, GLA_HEADS, GLA_DK, GLA_DV)),
        'c': nrm(ks[6], (DEC_BATCH, D_MODEL)),
        'c_ctx': nrm(ks[7], (D_MODEL,)),
        'w_ada': nrm(ks[8], (DEPTH, D_MODEL, 6 * D_MODEL), D_MODEL ** -0.5),
        'b_ada': nrm(ks[9], (DEPTH, 6 * D_MODEL), 0.02),
        'w_in': nrm(ks[10], (DEPTH, D_MODEL, IN_WIDTH), D_MODEL ** -0.5),
        'attn_sink': nrm(ks[11], (DEPTH, ATTN_HEADS), 0.5),
        'w_gate_f': nrm(ks[12], (DEPTH, GATE_RANK, GLA_KEY_WIDTH), GATE_RANK ** -0.5),
        'b_gate_f': nrm(ks[13], (DEPTH, GLA_KEY_WIDTH), 0.1),
        'w_gate_b': nrm(ks[14], (DEPTH, GATE_RANK, GLA_KEY_WIDTH), GATE_RANK ** -0.5),
        'b_gate_b': nrm(ks[15], (DEPTH, GLA_KEY_WIDTH), 0.1),
        'gla_norm_w': 1.0 + nrm(ks[16], (DEPTH, GLA_DV), 0.02),
        'w_out': nrm(ks[17], (DEPTH, D_MODEL, D_MODEL), D_MODEL ** -0.5 * DEEPNORM_BETA),
        'ln1_w': 1.0 + nrm(ks[18], (DEPTH, D_MODEL), 0.02),
        'ln1_b': nrm(ks[19], (DEPTH, D_MODEL), 0.02),
        'w_up': nrm(ks[20], (DEPTH, D_MODEL, 2 * D_FF), D_MODEL ** -0.5),
        'conv_w': nrm(ks[21], (DEPTH, CONV_WIDTH, 2 * D_FF), CONV_WIDTH ** -0.5),
        'conv_b': nrm(ks[22], (DEPTH, 2 * D_FF), 0.02),
        'w_down': nrm(ks[23], (DEPTH, D_FF, D_MODEL), D_FF ** -0.5 * DEEPNORM_BETA),
        'ln2_w': 1.0 + nrm(ks[24], (DEPTH, D_MODEL), 0.02),
        'ln2_b': nrm(ks[25], (DEPTH, D_MODEL), 0.02),
    }


def reference(x_prompt, x_sample, cache_k, cache_v, state_gla_fwd, state_gla_bwd, c, c_ctx,
              w_ada, b_ada, w_in, attn_sink, w_gate_f, b_gate_f, w_gate_b, b_gate_b, gla_norm_w,
              w_out, ln1_w, ln1_b, w_up, conv_w, conv_b, w_down, ln2_w, ln2_b):
    xp, xs = x_prompt, x_sample
    new_k, new_v, new_sf, new_sb = [], [], [], []
    for l in range(DEPTH):
        p = {'w_in': w_in[l], 'attn_sink': attn_sink[l], 'w_gate_f': w_gate_f[l], 'b_gate_f': b_gate_f[l],
             'w_gate_b': w_gate_b[l], 'b_gate_b': b_gate_b[l], 'gla_norm_w': gla_norm_w[l], 'w_out': w_out[l],
             'ln1_w': ln1_w[l], 'ln1_b': ln1_b[l], 'w_up': w_up[l], 'conv_w': conv_w[l], 'conv_b': conv_b[l],
             'w_down': w_down[l], 'ln2_w': ln2_w[l], 'ln2_b': ln2_b[l]}
        mods_ctx = (jax.nn.silu(c_ctx) @ w_ada[l] + b_ada[l])[None, None, :]
        s_zero = jnp.zeros((xp.shape[0], GLA_HEADS, GLA_DK, GLA_DV), jnp.float32)
        xp, k_l, v_l, sf_l, sb_l = _layer(xp, mods_ctx, p, s_zero, s_zero, None)
        new_k.append(k_l)
        new_v.append(v_l)
        new_sf.append(sf_l)
        new_sb.append(sb_l)
        mods_lat = (jax.nn.silu(c) @ w_ada[l] + b_ada[l])[:, None, :]
        xs = _layer(xs, mods_lat, p, state_gla_fwd[:, l], state_gla_bwd[:, l], (cache_k[:, l], cache_v[:, l]))[0]
    return (xp, xs, jnp.stack(new_k, axis=1), jnp.stack(new_v, axis=1), jnp.stack(new_sf, axis=1), jnp.stack(new_sb, axis=1))
```

```python
import functools

import jax
import jax.numpy as jnp
from jax import lax
from jax.experimental import pallas as pl
from jax.experimental.pallas import tpu as pltpu

F32 = jnp.float32
BF16 = jnp.bfloat16

D_MODEL = 2048
BATCH = 32
SEQ = 256
DEPTH = 2
DEC_BATCH = 4
DEC_SEQ = 2048
PAST_LEN = 256
GRID_W = 64
HEAD_DIM = 128
ATTN_WIDTH = 1024
ATTN_HEADS = 8
ATTN_KV_HEADS = 2
ATTN_GROUP = 4
KV_WIDTH = 256
WINDOW = 128
BLOCK = 128
GLA_WIDTH = 1024
GLA_HEADS = 4
GLA_DV = 256
GLA_DK = 128
GLA_KEY_WIDTH = 512
GATE_RANK = 16
GATE_NORM = 16.0
GLA_CHUNK = 64
D_FF = 5632
ROPE_THETA = 10000.0
LN_EPS = 1e-5
NEG_INF = -1e30
DEEPNORM_ALPHA = (2 * DEPTH) ** 0.25

N_CTX = BATCH * SEQ
N_LAT = DEC_BATCH * DEC_SEQ
N_TOK = N_CTX + N_LAT
MOD_ROWS = 8
N_MODS = 6

PROJ_MAIN = 4608
COL_K = ATTN_WIDTH
COL_V = COL_K + KV_WIDTH
COL_GQ = COL_V + KV_WIDTH
COL_GK = COL_GQ + GLA_KEY_WIDTH
COL_GV = COL_GK + GLA_KEY_WIDTH
COL_GOG = COL_GV + GLA_WIDTH
LR_PAD = 128

VMEM_LIMIT = 56 * 1024 * 1024

TM_PROJ = 1024
TN_PROJ = 1536
TM_OUT = 512
TM_FFN = 512
TF_FFN = 512
HALO = 16
TN_ADA = 1024


def _params(sem):
    return pltpu.CompilerParams(dimension_semantics=sem, vmem_limit_bytes=VMEM_LIMIT)


def _silu(x):
    return x / (1.0 + jnp.exp(-x))


def _layer_norm(y, w, b):
    mu = jnp.mean(y, axis=-1, keepdims=True)
    d = y - mu
    var = jnp.mean(d * d, axis=-1, keepdims=True)
    return d * lax.rsqrt(var + LN_EPS) * w + b


def _dot(a, b):
    return jnp.dot(a, b, preferred_element_type=F32)


def _dot_nt(a, b):
    return lax.dot_general(a, b, (((1,), (1,)), ((), ())), preferred_element_type=F32)


def _dot_tn(a, b):
    return lax.dot_general(a, b, (((0,), (0,)), ((), ())), preferred_element_type=F32)


def _mod_row(i, tm):
    n_ctx_tiles = N_CTX // tm
    return jnp.where(i < n_ctx_tiles, 0, 1 + (i - n_ctx_tiles) // (DEC_SEQ // tm))


def _mod_spec(layer, chunk, tm):
    def index_map(i, *_):
        return ((layer * MOD_ROWS + _mod_row(i, tm)) * N_MODS + chunk, 0, 0)
    return pl.BlockSpec((1, 1, D_MODEL), index_map)


def _ada_kernel(c_ref, w_ref, b_ref, o_ref):
    s = _silu(c_ref[...]).astype(BF16)
    o_ref[0] = _dot(s, w_ref[0].astype(BF16)) + b_ref[0]


def _ada(cc, w_ada, b_ada):
    n_out = N_MODS * D_MODEL
    return pl.pallas_call(
        _ada_kernel,
        out_shape=jax.ShapeDtypeStruct((DEPTH, MOD_ROWS, n_out), F32),
        grid=(DEPTH, n_out // TN_ADA),
        in_specs=[
            pl.BlockSpec((MOD_ROWS, D_MODEL), lambda l, j: (0, 0)),
            pl.BlockSpec((1, D_MODEL, TN_ADA), lambda l, j: (l, 0, j)),
            pl.BlockSpec((1, 1, TN_ADA), lambda l, j: (l, 0, j)),
        ],
        out_specs=pl.BlockSpec((1, MOD_ROWS, TN_ADA), lambda l, j: (l, 0, j)),
        compiler_params=_params(("arbitrary", "arbitrary")),
        name="ada",
    )(cc, w_ada, b_ada.reshape(DEPTH, 1, n_out))


def _inproj_kernel(x_ref, sh_ref, sc_ref, w_ref, wlr_ref, o_ref, olr_ref, h_scr):
    @pl.when(pl.program_id(1) == 0)
    def _():
        h = (x_ref[...] * (1.0 + sc_ref[0]) + sh_ref[0]).astype(BF16)
        h_scr[...] = h
        olr_ref[...] = _dot(h, wlr_ref[...])
    o_ref[...] = _dot(h_scr[...], w_ref[...])


def _in_proj(x, mods, w_main, w_lr, layer):
    tm, tn = TM_PROJ, TN_PROJ
    return pl.pallas_call(
        _inproj_kernel,
        out_shape=(jax.ShapeDtypeStruct((N_TOK, PROJ_MAIN), F32),
                   jax.ShapeDtypeStruct((N_TOK, LR_PAD), F32)),
        grid=(N_TOK // tm, PROJ_MAIN // tn),
        in_specs=[
            pl.BlockSpec((tm, D_MODEL), lambda i, j: (i, 0)),
            _mod_spec(layer, 0, tm),
            _mod_spec(layer, 1, tm),
            pl.BlockSpec((D_MODEL, tn), lambda i, j: (0, j)),
            pl.BlockSpec((D_MODEL, LR_PAD), lambda i, j: (0, 0)),
        ],
        out_specs=(pl.BlockSpec((tm, tn), lambda i, j: (i, j)),
                   pl.BlockSpec((tm, LR_PAD), lambda i, j: (i, 0))),
        scratch_shapes=[pltpu.VMEM((tm, D_MODEL), BF16)],
        compiler_params=_params(("arbitrary", "arbitrary")),
        name="in_proj",
    )(x, mods, mods, w_main, w_lr)


def _sink_softmax_pv(sk, parts):
    m = sk
    for s, _ in parts:
        m = jnp.maximum(m, jnp.max(s, axis=-1, keepdims=True))
    es = [jnp.exp(s - m) for s, _ in parts]
    den = jnp.exp(sk - m)
    for e in es:
        den = den + jnp.sum(e, axis=-1, keepdims=True)
    inv = 1.0 / den
    out = None
    for e, (_, v) in zip(es, parts):
        o = _dot((e * inv).astype(BF16), v)
        out = o if out is None else out + o
    return out


def _attn_ctx_kernel(sink_ref, q_ref, k_ref, v_ref, o_ref):
    scale = HEAD_DIM ** -0.5
    for kh in range(ATTN_KV_HEADS):
        cols = slice(kh * HEAD_DIM, (kh + 1) * HEAD_DIM)
        k = k_ref[:, cols].astype(BF16)
        v = v_ref[:, cols].astype(BF16)
        for g in range(ATTN_GROUP):
            h = kh * ATTN_GROUP + g
            hc = slice(h * HEAD_DIM, (h + 1) * HEAD_DIM)
            s = _dot_nt(q_ref[:, hc].astype(BF16), k) * scale
            o_ref[:, hc] = _sink_softmax_pv(sink_ref[h], [(s, v)]).astype(BF16)


def _attn_ctx(proj, sink):
    return pl.pallas_call(
        _attn_ctx_kernel,
        out_shape=jax.ShapeDtypeStruct((N_CTX, ATTN_WIDTH), BF16),
        grid=(BATCH,),
        in_specs=[
            pl.BlockSpec(memory_space=pltpu.SMEM),
            pl.BlockSpec((SEQ, ATTN_WIDTH), lambda b: (b, 0)),
            pl.BlockSpec((SEQ, KV_WIDTH), lambda b: (b, COL_K // KV_WIDTH)),
            pl.BlockSpec((SEQ, KV_WIDTH), lambda b: (b, COL_V // KV_WIDTH)),
        ],
        out_specs=pl.BlockSpec((SEQ, ATTN_WIDTH), lambda b: (b, 0)),
        compiler_params=_params(("arbitrary",)),
        name="attn_ctx",
    )(sink, proj, proj, proj)


def _attn_lat_kernel(sink_ref, q_ref, k_ref, v_ref, ck_ref, cv_ref, cos_ref, sin_ref, o_ref):
    scale = HEAD_DIM ** -0.5
    win = 3 * BLOCK
    i = pl.program_id(1)
    q0 = pl.multiple_of(i * BLOCK, BLOCK)
    start = pl.multiple_of(jnp.clip((i - 1) * BLOCK, 0, DEC_SEQ - win), BLOCK)
    cos_q, sin_q = cos_ref[pl.ds(q0, BLOCK), :], sin_ref[pl.ds(q0, BLOCK), :]
    cos_k, sin_k = cos_ref[pl.ds(start, win), :], sin_ref[pl.ds(start, win), :]
    lane = lax.broadcasted_iota(jnp.int32, (1, HEAD_DIM), 1)
    low = (lane % (HEAD_DIM // 2)) < (HEAD_DIM // 4)

    def rope(x, c, s):
        partner = jnp.where(low, pltpu.roll(x, HEAD_DIM - HEAD_DIM // 4, 1),
                            pltpu.roll(x, HEAD_DIM // 4, 1))
        return x * c + partner * s

    qpos = q0 + lax.broadcasted_iota(jnp.int32, (BLOCK, 1), 0)
    kpos = start + lax.broadcasted_iota(jnp.int32, (1, win), 1)
    valid = jnp.abs(qpos - kpos) <= WINDOW

    for kh in range(ATTN_KV_HEADS):
        cols = slice(kh * HEAD_DIM, (kh + 1) * HEAD_DIM)
        kw = rope(k_ref[pl.ds(start, win), cols], cos_k, sin_k).astype(BF16)
        vw = v_ref[pl.ds(start, win), cols].astype(BF16)
        ck = ck_ref[0, 0, :, cols].astype(BF16)
        cv = cv_ref[0, 0, :, cols].astype(BF16)
        for g in range(ATTN_GROUP):
            h = kh * ATTN_GROUP + g
            hc = slice(h * HEAD_DIM, (h + 1) * HEAD_DIM)
            q = rope(q_ref[:, hc], cos_q, sin_q).astype(BF16)
            s_loc = jnp.where(valid, _dot_nt(q, kw) * scale, NEG_INF)
            s_ctx = _dot_nt(q, ck) * scale
            o_ref[:, hc] = _sink_softmax_pv(sink_ref[h], [(s_loc, vw), (s_ctx, cv)]).astype(BF16)


def _attn_lat(proj, sink, cache_k, cache_v, cos_t, sin_t, layer):
    nb = DEC_SEQ // BLOCK
    row0 = N_CTX // DEC_SEQ
    cache_spec = pl.BlockSpec((1, 1, PAST_LEN, KV_WIDTH), lambda b, i: (b, layer, 0, 0))
    table_spec = pl.BlockSpec((DEC_SEQ, HEAD_DIM), lambda b, i: (0, 0))
    return pl.pallas_call(
        _attn_lat_kernel,
        out_shape=jax.ShapeDtypeStruct((N_LAT, ATTN_WIDTH), BF16),
        grid=(DEC_BATCH, nb),
        in_specs=[
            pl.BlockSpec(memory_space=pltpu.SMEM),
            pl.BlockSpec((BLOCK, ATTN_WIDTH), lambda b, i: (N_CTX // BLOCK + b * nb + i, 0)),
            pl.BlockSpec((DEC_SEQ, KV_WIDTH), lambda b, i: (row0 + b, COL_K // KV_WIDTH)),
            pl.BlockSpec((DEC_SEQ, KV_WIDTH), lambda b, i: (row0 + b, COL_V // KV_WIDTH)),
            cache_spec, cache_spec, table_spec, table_spec,
        ],
        out_specs=pl.BlockSpec((BLOCK, ATTN_WIDTH), lambda b, i: (b * nb + i, 0)),
        compiler_params=_params(("arbitrary", "arbitrary")),
        name="attn_lat",
    )(sink, proj, proj, proj,
      cache_k.reshape(DEC_BATCH, DEPTH, PAST_LEN, KV_WIDTH),
      cache_v.reshape(DEC_BATCH, DEPTH, PAST_LEN, KV_WIDTH), cos_t, sin_t)


def _rope_tables():
    half = HEAD_DIM // 2
    n_freq = half // 2
    t = jnp.arange(DEC_SEQ)
    row = (t // GRID_W).astype(F32)
    col = (t % GRID_W).astype(F32)
    freqs = ROPE_THETA ** (-jnp.arange(n_freq, dtype=F32) / n_freq)
    tabs = []
    for pos in (row, col):
        ang = pos[:, None] * freqs[None, :]
        tabs.append((jnp.cos(ang), jnp.sin(ang)))
    cos_t = jnp.concatenate([tabs[0][0], tabs[0][0], tabs[1][0], tabs[1][0]], axis=-1)
    sin_t = jnp.concatenate([-tabs[0][1], tabs[0][1], -tabs[1][1], tabs[1][1]], axis=-1)
    return cos_t, sin_t


def _log_sigmoid(x):
    return jnp.minimum(x, 0.0) - jnp.log1p(jnp.exp(-jnp.abs(x)))


def _gla_kernel(*refs, seq, zero_init):
    if zero_init:
        (q_ref, k_ref, v_ref, og_ref, lr_ref, wgf_ref, wgb_ref, bgf_ref, bgb_ref, nw_ref,
         o_ref, sf_ref, sb_ref, gf_scr, gb_scr, of_scr, ob_scr, stf_scr, stb_scr) = refs
    else:
        (q_ref, k_ref, v_ref, og_ref, lr_ref, wgf_ref, wgb_ref, bgf_ref, bgb_ref, nw_ref,
         s0f_ref, s0b_ref,
         o_ref, sf_ref, sb_ref, gf_scr, gb_scr, of_scr, ob_scr, stf_scr, stb_scr) = refs
    n_chunks = seq // GLA_CHUNK
    c = GLA_CHUNK

    lr = lr_ref[...].astype(BF16)
    gf_scr[...] = _log_sigmoid(_dot(lr, wgf_ref[...]) + bgf_ref[...]) / GATE_NORM
    gb_scr[...] = _log_sigmoid(_dot(lr, wgb_ref[...]) + bgb_ref[...]) / GATE_NORM

    if zero_init:
        stf_scr[...] = jnp.zeros_like(stf_scr)
        stb_scr[...] = jnp.zeros_like(stb_scr)
    else:
        stf_scr[...] = s0f_ref[0, 0].T
        stb_scr[...] = s0b_ref[0, 0].T

    ri = lax.broadcasted_iota(jnp.int32, (c, c), 0)
    ci = lax.broadcasted_iota(jnp.int32, (c, c), 1)
    lower = ri >= ci
    tri_lo = lower.astype(F32)
    tri_up = (ri <= ci).astype(F32)
    q_scale = GLA_DK ** -0.5

    def direction(rows, g_scr, tri, keep, edge, st_scr, out_scr):
        q = q_ref[rows, :] * q_scale
        k = k_ref[rows, :]
        v = v_ref[rows, :].astype(BF16)
        b = jnp.dot(tri, g_scr[rows, :], preferred_element_type=F32,
                    precision=lax.Precision.HIGHEST)
        qe = (q * jnp.exp(b)).astype(BF16)
        ke = k * jnp.exp(-b)
        decay = jnp.exp(b[edge:edge + 1, :])
        kd = (ke * decay).astype(BF16)
        a = jnp.where(keep, _dot_nt(qe, ke.astype(BF16)), 0.0).astype(BF16)
        st = st_scr[...]
        out_scr[rows, :] = _dot(a, v) + _dot_nt(qe, st.astype(BF16))
        st_scr[...] = decay * st + _dot_tn(v, kd)

    def body(n, carry):
        rows_f = pl.ds(pl.multiple_of(n * c, c), c)
        rows_b = pl.ds(pl.multiple_of((n_chunks - 1 - n) * c, c), c)
        direction(rows_f, gf_scr, tri_lo, lower, c - 1, stf_scr, of_scr)
        direction(rows_b, gb_scr, tri_up, ri <= ci, 0, stb_scr, ob_scr)
        return carry

    lax.fori_loop(0, n_chunks, body, 0)

    sf_ref[0, 0] = stf_scr[...].T
    sb_ref[0, 0] = stb_scr[...].T

    nw = nw_ref[...]
    blk = 256

    def finish(t, carry):
        rows = pl.ds(pl.multiple_of(t * blk, blk), blk)
        o = of_scr[rows, :] + ob_scr[rows, :]
        o = o * lax.rsqrt(jnp.mean(o * o, axis=-1, keepdims=True) + LN_EPS) * nw
        o_ref[rows, :] = (o * _silu(og_ref[rows, :])).astype(BF16)
        return carry

    lax.fori_loop(0, seq // blk, finish, 0)


def _gla(proj, proj_lr, wgf, wgb, bgf, bgb, norm_w, s0f, s0b, *, seq, n_batch, row0):
    zero_init = s0f is None

    def col_spec(width, col):
        return pl.BlockSpec((seq, width), lambda b, h: (row0 + b, col // width + h))

    head_w = pl.BlockSpec((LR_PAD, GLA_DK), lambda b, h: (0, h))
    head_b = pl.BlockSpec((1, GLA_DK), lambda b, h: (0, h))
    state_spec = pl.BlockSpec((1, 1, GLA_DK, GLA_DV), lambda b, h: (b, h, 0, 0))
    in_specs = [
        col_spec(GLA_DK, COL_GQ), col_spec(GLA_DK, COL_GK), col_spec(GLA_DV, COL_GV),
        col_spec(GLA_DV, COL_GOG),
        pl.BlockSpec((seq, LR_PAD), lambda b, h: (row0 + b, 0)),
        head_w, head_w, head_b, head_b,
        pl.BlockSpec((1, GLA_DV), lambda b, h: (0, 0)),
    ]
    args = [proj, proj, proj, proj, proj_lr, wgf, wgb, bgf, bgb, norm_w]
    if not zero_init:
        in_specs += [state_spec, state_spec]
        args += [s0f, s0b]
    state_shape = jax.ShapeDtypeStruct((n_batch, GLA_HEADS, GLA_DK, GLA_DV), F32)
    return pl.pallas_call(
        functools.partial(_gla_kernel, seq=seq, zero_init=zero_init),
        out_shape=(jax.ShapeDtypeStruct((n_batch * seq, GLA_WIDTH), BF16), state_shape, state_shape),
        grid=(n_batch, GLA_HEADS),
        in_specs=in_specs,
        out_specs=(pl.BlockSpec((seq, GLA_DV), lambda b, h: (b, h)), state_spec, state_spec),
        scratch_shapes=[
            pltpu.VMEM((seq, GLA_DK), F32), pltpu.VMEM((seq, GLA_DK), F32),
            pltpu.VMEM((seq, GLA_DV), F32), pltpu.VMEM((seq, GLA_DV), F32),
            pltpu.VMEM((GLA_DV, GLA_DK), F32), pltpu.VMEM((GLA_DV, GLA_DK), F32),
        ],
        compiler_params=_params(("arbitrary", "arbitrary")),
        name="gla_ctx" if zero_init else "gla_lat",
    )(*args)


def _outproj_kernel(oa_ref, og_ref, x_ref, g_ref, w_ref, lw_ref, lb_ref, o_ref):
    f = _dot(oa_ref[...], w_ref[:ATTN_WIDTH, :]) + _dot(og_ref[...], w_ref[ATTN_WIDTH:, :])
    y = DEEPNORM_ALPHA * x_ref[...] + g_ref[0] * f
    o_ref[...] = _layer_norm(y, lw_ref[...], lb_ref[...])


def _out_proj(o_att, o_gla, x, mods, w_out, ln_w, ln_b, layer):
    tm = TM_OUT
    vec = pl.BlockSpec((1, D_MODEL), lambda i: (0, 0))
    return pl.pallas_call(
        _outproj_kernel,
        out_shape=jax.ShapeDtypeStruct((N_TOK, D_MODEL), F32),
        grid=(N_TOK // tm,),
        in_specs=[
            pl.BlockSpec((tm, ATTN_WIDTH), lambda i: (i, 0)),
            pl.BlockSpec((tm, GLA_WIDTH), lambda i: (i, 0)),
            pl.BlockSpec((tm, D_MODEL), lambda i: (i, 0)),
            _mod_spec(layer, 2, tm),
            pl.BlockSpec((D_MODEL, D_MODEL), lambda i: (0, 0)),
            vec, vec,
        ],
        out_specs=pl.BlockSpec((tm, D_MODEL), lambda i: (i, 0)),
        compiler_params=_params(("arbitrary",)),
        name="out_proj",
    )(o_att, o_gla, x, mods, w_out, ln_w, ln_b)


def _ffn_kernel(x_ref, xp_ref, xn_ref, sh_ref, sc_ref, g_ref, wa_ref, wg_ref, cwa_ref, cwg_ref,
                cba_ref, cbg_ref, wd_ref, lw_ref, lb_ref, o_ref, h_scr, *, tm):
    i = pl.program_id(0)
    j = pl.program_id(1)
    rows = tm + 2 * HALO

    @pl.when(j == 0)
    def _():
        sc = 1.0 + sc_ref[0]
        sh = sh_ref[0]
        h_scr[0:HALO, :] = (xp_ref[...] * sc + sh).astype(BF16)
        h_scr[HALO:HALO + tm, :] = (x_ref[...] * sc + sh).astype(BF16)
        h_scr[HALO + tm:rows, :] = (xn_ref[...] * sc + sh).astype(BF16)

    h = h_scr[...]
    seq_mask = jnp.where(i < N_CTX // tm, SEQ - 1, DEC_SEQ - 1)
    pos = (i * tm + lax.broadcasted_iota(jnp.int32, (tm, 1), 0)) & seq_mask
    first = pos == 0
    last = pos == seq_mask

    def conv(u, cw_ref, cb_ref):
        cw = cw_ref[...]
        prev = pltpu.roll(u, 1, 0)[HALO:HALO + tm]
        nxt = pltpu.roll(u, rows - 1, 0)[HALO:HALO + tm]
        return (jnp.where(first, 0.0, prev) * cw[0:1] + u[HALO:HALO + tm] * cw[1:2]
                + jnp.where(last, 0.0, nxt) * cw[2:3] + cb_ref[...])

    a = conv(_dot(h, wa_ref[...]), cwa_ref, cba_ref)
    g = conv(_dot(h, wg_ref[...]), cwg_ref, cbg_ref)
    contrib = _dot((_silu(g) * a).astype(BF16), wd_ref[...])

    @pl.when(j == 0)
    def _():
        o_ref[...] = contrib

    @pl.when(j > 0)
    def _():
        o_ref[...] += contrib

    @pl.when(j == pl.num_programs(1) - 1)
    def _():
        y = DEEPNORM_ALPHA * x_ref[...] + g_ref[0] * o_ref[...]
        o_ref[...] = _layer_norm(y, lw_ref[...], lb_ref[...])


def _ffn(x, mods, w_up, conv_w, conv_b, w_down, ln_w, ln_b, layer):
    tm, tf = TM_FFN, TF_FFN
    nf = D_FF // tf
    hb = tm // HALO
    n_halo_blocks = N_TOK // HALO
    vec = pl.BlockSpec((1, D_MODEL), lambda i, j: (0, 0))
    return pl.pallas_call(
        functools.partial(_ffn_kernel, tm=tm),
        out_shape=jax.ShapeDtypeStruct((N_TOK, D_MODEL), F32),
        grid=(N_TOK // tm, nf),
        in_specs=[
            pl.BlockSpec((tm, D_MODEL), lambda i, j: (i, 0)),
            pl.BlockSpec((HALO, D_MODEL), lambda i, j: (jnp.maximum(i * hb - 1, 0), 0)),
            pl.BlockSpec((HALO, D_MODEL),
                         lambda i, j: (jnp.minimum((i + 1) * hb, n_halo_blocks - 1), 0)),
            _mod_spec(layer, 3, tm), _mod_spec(layer, 4, tm), _mod_spec(layer, 5, tm),
            pl.BlockSpec((D_MODEL, tf), lambda i, j: (0, j)),
            pl.BlockSpec((D_MODEL, tf), lambda i, j: (0, nf + j)),
            pl.BlockSpec((3, tf), lambda i, j: (0, j)),
            pl.BlockSpec((3, tf), lambda i, j: (0, nf + j)),
            pl.BlockSpec((1, tf), lambda i, j: (0, j)),
            pl.BlockSpec((1, tf), lambda i, j: (0, nf + j)),
            pl.BlockSpec((tf, D_MODEL), lambda i, j: (j, 0)),
            vec, vec,
        ],
        out_specs=pl.BlockSpec((tm, D_MODEL), lambda i, j: (i, 0)),
        scratch_shapes=[pltpu.VMEM((tm + 2 * HALO, D_MODEL), BF16)],
        compiler_params=_params(("arbitrary", "arbitrary")),
        name="ffn",
    )(x, x, x, mods, mods, mods, w_up, w_up, conv_w, conv_w, conv_b, conv_b, w_down, ln_w, ln_b)


def _pad_gate(w, row0):
    return jnp.zeros((LR_PAD, GLA_KEY_WIDTH), F32).at[row0:row0 + GATE_RANK].set(w).astype(BF16)


def kernel(x_prompt, x_sample, cache_k, cache_v, state_gla_fwd, state_gla_bwd, c, c_ctx, w_ada,
           b_ada, w_in, attn_sink, w_gate_f, b_gate_f, w_gate_b, b_gate_b, gla_norm_w, w_out,
           ln1_w, ln1_b, w_up, conv_w, conv_b, w_down, ln2_w, ln2_b):
    x = jnp.concatenate([x_prompt.reshape(N_CTX, D_MODEL), x_sample.reshape(N_LAT, D_MODEL)], axis=0)
    cc = jnp.concatenate([c_ctx[None, :], c, jnp.zeros((MOD_ROWS - 1 - DEC_BATCH, D_MODEL), F32)], axis=0)
    mods = _ada(cc, w_ada, b_ada).reshape(DEPTH * MOD_ROWS * N_MODS, 1, D_MODEL)
    cos_t, sin_t = _rope_tables()

    new_k, new_v, new_sf, new_sb = [], [], [], []
    for l in range(DEPTH):
        w_main = w_in[l, :, :PROJ_MAIN].astype(BF16)
        w_lr = jnp.pad(w_in[l, :, PROJ_MAIN:], ((0, 0), (0, LR_PAD - 2 * GATE_RANK))).astype(BF16)
        proj, proj_lr = _in_proj(x, mods, w_main, w_lr, l)

        o_att = jnp.concatenate([
            _attn_ctx(proj, attn_sink[l]),
            _attn_lat(proj, attn_sink[l], cache_k, cache_v, cos_t, sin_t, l)], axis=0)

        wgf = _pad_gate(w_gate_f[l], 0)
        wgb = _pad_gate(w_gate_b[l], GATE_RANK)
        bgf = b_gate_f[l][None, :]
        bgb = b_gate_b[l][None, :]
        norm_w = gla_norm_w[l][None, :]
        og_ctx, sf, sb = _gla(proj, proj_lr, wgf, wgb, bgf, bgb, norm_w, None, None,
                              seq=SEQ, n_batch=BATCH, row0=0)
        og_lat, _, _ = _gla(proj, proj_lr, wgf, wgb, bgf, bgb, norm_w,
                            state_gla_fwd[:, l], state_gla_bwd[:, l],
                            seq=DEC_SEQ, n_batch=DEC_BATCH, row0=N_CTX // DEC_SEQ)
        o_gla = jnp.concatenate([og_ctx, og_lat], axis=0)

        x = _out_proj(o_att, o_gla, x, mods, w_out[l].astype(BF16), ln1_w[l][None, :],
                      ln1_b[l][None, :], l)
        x = _ffn(x, mods, w_up[l].astype(BF16), conv_w[l], conv_b[l][None, :],
                 w_down[l].astype(BF16), ln2_w[l][None, :], ln2_b[l][None, :], l)

        new_k.append(proj[:N_CTX, COL_K:COL_V].reshape(BATCH, SEQ, ATTN_KV_HEADS, HEAD_DIM))
        new_v.append(proj[:N_CTX, COL_V:COL_GQ].reshape(BATCH, SEQ, ATTN_KV_HEADS, HEAD_DIM))
        new_sf.append(sf)
        new_sb.append(sb)

    return (x[:N_CTX].reshape(BATCH, SEQ, D_MODEL), x[N_CTX:].reshape(DEC_BATCH, DEC_SEQ, D_MODEL),
            jnp.stack(new_k, axis=1), jnp.stack(new_v, axis=1),
            jnp.stack(new_sf, axis=1), jnp.stack(new_sb, axis=1))
```

```python
import functools

import jax
import jax.numpy as jnp
from jax import lax
from jax.experimental import pallas as pl
from jax.experimental.pallas import tpu as pltpu

F32 = jnp.float32
BF16 = jnp.bfloat16

D_MODEL = 2048
BATCH = 32
SEQ = 256
DEPTH = 2
DEC_BATCH = 4
DEC_SEQ = 2048
PAST_LEN = 256
GRID_W = 64
HEAD_DIM = 128
ATTN_WIDTH = 1024
ATTN_HEADS = 8
ATTN_KV_HEADS = 2
ATTN_GROUP = 4
KV_WIDTH = 256
WINDOW = 128
BLOCK = 128
GLA_WIDTH = 1024
GLA_HEADS = 4
GLA_DV = 256
GLA_DK = 128
GLA_KEY_WIDTH = 512
GATE_RANK = 16
GATE_NORM = 16.0
GLA_CHUNK = 64
GLA_GROUP = 256
D_FF = 5632
ROPE_THETA = 10000.0
LN_EPS = 1e-5
NEG_INF = -1e30
DEEPNORM_ALPHA = (2 * DEPTH) ** 0.25

N_CTX = BATCH * SEQ
N_LAT = DEC_BATCH * DEC_SEQ
N_TOK = N_CTX + N_LAT
MOD_ROWS = 8
N_MODS = 6

PROJ_MAIN = 4608
COL_GV = ATTN_WIDTH
COL_GOG = COL_GV + GLA_WIDTH
COL_GQ = COL_GOG + GLA_WIDTH
COL_GK = COL_GQ + GLA_KEY_WIDTH
COL_K = COL_GK + GLA_KEY_WIDTH
COL_V = COL_K + KV_WIDTH
LR_PAD = 128
W_IN_ORDER = ((0, 1024), (2560, 3584), (3584, 4608), (1536, 2048), (2048, 2560), (1024, 1280),
              (1280, 1536))

VMEM_LIMIT = 56 * 1024 * 1024

TM_PROJ = 1024
TN_PROJ = 1536
TM_OUT = 512
TM_FFN = 512
TF_FFN = 512
HALO = 16
TN_ADA = 1024


def _params(sem):
    return pltpu.CompilerParams(dimension_semantics=sem, vmem_limit_bytes=VMEM_LIMIT)


def _silu(x):
    return x / (1.0 + jnp.exp(-x))


def _layer_norm(y, w, b):
    mu = jnp.mean(y, axis=-1, keepdims=True)
    d = y - mu
    var = jnp.mean(d * d, axis=-1, keepdims=True)
    return d * lax.rsqrt(var + LN_EPS) * w + b


def _dot(a, b):
    return jnp.dot(a, b, preferred_element_type=F32)


def _dot_nt(a, b):
    return lax.dot_general(a, b, (((1,), (1,)), ((), ())), preferred_element_type=F32)


def _dot_tn(a, b):
    return lax.dot_general(a, b, (((0,), (0,)), ((), ())), preferred_element_type=F32)


def _mod_row(i, tm):
    n_ctx_tiles = N_CTX // tm
    return jnp.where(i < n_ctx_tiles, 0, 1 + (i - n_ctx_tiles) // (DEC_SEQ // tm))


def _mod_spec(layer, chunk, tm):
    def index_map(i, *_):
        return ((layer * MOD_ROWS + _mod_row(i, tm)) * N_MODS + chunk, 0, 0)
    return pl.BlockSpec((1, 1, D_MODEL), index_map)


def _ada_kernel(c_ref, w_ref, b_ref, o_ref):
    s = _silu(c_ref[...]).astype(BF16)
    o_ref[0] = _dot(s, w_ref[0].astype(BF16)) + b_ref[0]


def _ada(cc, w_ada, b_ada):
    n_out = N_MODS * D_MODEL
    return pl.pallas_call(
        _ada_kernel,
        out_shape=jax.ShapeDtypeStruct((DEPTH, MOD_ROWS, n_out), F32),
        grid=(DEPTH, n_out // TN_ADA),
        in_specs=[
            pl.BlockSpec((MOD_ROWS, D_MODEL), lambda l, j: (0, 0)),
            pl.BlockSpec((1, D_MODEL, TN_ADA), lambda l, j: (l, 0, j)),
            pl.BlockSpec((1, 1, TN_ADA), lambda l, j: (l, 0, j)),
        ],
        out_specs=pl.BlockSpec((1, MOD_ROWS, TN_ADA), lambda l, j: (l, 0, j)),
        compiler_params=_params(("arbitrary", "arbitrary")),
        name="ada",
    )(cc, w_ada, b_ada.reshape(DEPTH, 1, n_out))


def _inproj_kernel(x_ref, sh_ref, sc_ref, w_ref, wlr_ref, o_ref, olr_ref, h_scr):
    @pl.when(pl.program_id(1) == 0)
    def _():
        h = (x_ref[...] * (1.0 + sc_ref[0]) + sh_ref[0]).astype(BF16)
        h_scr[...] = h
        olr_ref[...] = _dot(h, wlr_ref[...])
    o_ref[...] = _dot(h_scr[...], w_ref[...])


def _in_proj(x, mods, w_main, w_lr, layer):
    tm, tn = TM_PROJ, TN_PROJ
    return pl.pallas_call(
        _inproj_kernel,
        out_shape=(jax.ShapeDtypeStruct((N_TOK, PROJ_MAIN), F32),
                   jax.ShapeDtypeStruct((N_TOK, LR_PAD), F32)),
        grid=(N_TOK // tm, PROJ_MAIN // tn),
        in_specs=[
            pl.BlockSpec((tm, D_MODEL), lambda i, j: (i, 0)),
            _mod_spec(layer, 0, tm),
            _mod_spec(layer, 1, tm),
            pl.BlockSpec((D_MODEL, tn), lambda i, j: (0, j)),
            pl.BlockSpec((D_MODEL, LR_PAD), lambda i, j: (0, 0)),
        ],
        out_specs=(pl.BlockSpec((tm, tn), lambda i, j: (i, j)),
                   pl.BlockSpec((tm, LR_PAD), lambda i, j: (i, 0))),
        scratch_shapes=[pltpu.VMEM((tm, D_MODEL), BF16)],
        compiler_params=_params(("arbitrary", "arbitrary")),
        name="in_proj",
    )(x, mods, mods, w_main, w_lr)


def _sink_softmax_pv(sk, parts):
    m = sk
    for s, _ in parts:
        m = jnp.maximum(m, jnp.max(s, axis=-1, keepdims=True))
    es = [jnp.exp(s - m) for s, _ in parts]
    den = jnp.exp(sk - m)
    for e in es:
        den = den + jnp.sum(e, axis=-1, keepdims=True)
    inv = 1.0 / den
    out = None
    for e, (_, v) in zip(es, parts):
        o = _dot((e * inv).astype(BF16), v)
        out = o if out is None else out + o
    return out


def _attn_ctx_kernel(sink_ref, q_ref, k_ref, v_ref, o_ref):
    scale = HEAD_DIM ** -0.5
    for kh in range(ATTN_KV_HEADS):
        cols = slice(kh * HEAD_DIM, (kh + 1) * HEAD_DIM)
        k = k_ref[:, cols].astype(BF16)
        v = v_ref[:, cols].astype(BF16)
        for g in range(ATTN_GROUP):
            h = kh * ATTN_GROUP + g
            hc = slice(h * HEAD_DIM, (h + 1) * HEAD_DIM)
            s = _dot_nt(q_ref[:, hc].astype(BF16), k) * scale
            o_ref[:, hc] = _sink_softmax_pv(sink_ref[h], [(s, v)]).astype(BF16)


def _attn_ctx(proj, sink):
    return pl.pallas_call(
        _attn_ctx_kernel,
        out_shape=jax.ShapeDtypeStruct((N_CTX, ATTN_WIDTH), BF16),
        grid=(BATCH,),
        in_specs=[
            pl.BlockSpec(memory_space=pltpu.SMEM),
            pl.BlockSpec((SEQ, ATTN_WIDTH), lambda b: (b, 0)),
            pl.BlockSpec((SEQ, KV_WIDTH), lambda b: (b, COL_K // KV_WIDTH)),
            pl.BlockSpec((SEQ, KV_WIDTH), lambda b: (b, COL_V // KV_WIDTH)),
        ],
        out_specs=pl.BlockSpec((SEQ, ATTN_WIDTH), lambda b: (b, 0)),
        compiler_params=_params(("arbitrary",)),
        name="attn_ctx",
    )(sink, proj, proj, proj)


def _attn_lat_kernel(sink_ref, q_ref, k_ref, v_ref, ck_ref, cv_ref, cos_ref, sin_ref, o_ref):
    scale = HEAD_DIM ** -0.5
    win = 3 * BLOCK
    i = pl.program_id(1)
    q0 = pl.multiple_of(i * BLOCK, BLOCK)
    start = pl.multiple_of(jnp.clip((i - 1) * BLOCK, 0, DEC_SEQ - win), BLOCK)
    cos_q, sin_q = cos_ref[pl.ds(q0, BLOCK), :], sin_ref[pl.ds(q0, BLOCK), :]
    cos_k, sin_k = cos_ref[pl.ds(start, win), :], sin_ref[pl.ds(start, win), :]
    lane = lax.broadcasted_iota(jnp.int32, (1, HEAD_DIM), 1)
    low = (lane % (HEAD_DIM // 2)) < (HEAD_DIM // 4)

    def rope(x, c, s):
        partner = jnp.where(low, pltpu.roll(x, HEAD_DIM - HEAD_DIM // 4, 1),
                            pltpu.roll(x, HEAD_DIM // 4, 1))
        return x * c + partner * s

    qpos = q0 + lax.broadcasted_iota(jnp.int32, (BLOCK, 1), 0)
    kpos = start + lax.broadcasted_iota(jnp.int32, (1, win), 1)
    valid = jnp.abs(qpos - kpos) <= WINDOW

    for kh in range(ATTN_KV_HEADS):
        cols = slice(kh * HEAD_DIM, (kh + 1) * HEAD_DIM)
        kw = rope(k_ref[pl.ds(start, win), cols], cos_k, sin_k).astype(BF16)
        vw = v_ref[pl.ds(start, win), cols].astype(BF16)
        ck = ck_ref[0, 0, :, cols].astype(BF16)
        cv = cv_ref[0, 0, :, cols].astype(BF16)
        for g in range(ATTN_GROUP):
            h = kh * ATTN_GROUP + g
            hc = slice(h * HEAD_DIM, (h + 1) * HEAD_DIM)
            q = rope(q_ref[:, hc], cos_q, sin_q).astype(BF16)
            s_loc = jnp.where(valid, _dot_nt(q, kw) * scale, NEG_INF)
            s_ctx = _dot_nt(q, ck) * scale
            o_ref[:, hc] = _sink_softmax_pv(sink_ref[h], [(s_loc, vw), (s_ctx, cv)]).astype(BF16)


def _attn_lat(proj, sink, cache_k, cache_v, cos_t, sin_t, layer):
    nb = DEC_SEQ // BLOCK
    row0 = N_CTX // DEC_SEQ
    cache_spec = pl.BlockSpec((1, 1, PAST_LEN, KV_WIDTH), lambda b, i: (b, layer, 0, 0))
    table_spec = pl.BlockSpec((DEC_SEQ, HEAD_DIM), lambda b, i: (0, 0))
    return pl.pallas_call(
        _attn_lat_kernel,
        out_shape=jax.ShapeDtypeStruct((N_LAT, ATTN_WIDTH), BF16),
        grid=(DEC_BATCH, nb),
        in_specs=[
            pl.BlockSpec(memory_space=pltpu.SMEM),
            pl.BlockSpec((BLOCK, ATTN_WIDTH), lambda b, i: (N_CTX // BLOCK + b * nb + i, 0)),
            pl.BlockSpec((DEC_SEQ, KV_WIDTH), lambda b, i: (row0 + b, COL_K // KV_WIDTH)),
            pl.BlockSpec((DEC_SEQ, KV_WIDTH), lambda b, i: (row0 + b, COL_V // KV_WIDTH)),
            cache_spec, cache_spec, table_spec, table_spec,
        ],
        out_specs=pl.BlockSpec((BLOCK, ATTN_WIDTH), lambda b, i: (b * nb + i, 0)),
        compiler_params=_params(("arbitrary", "arbitrary")),
        name="attn_lat",
    )(sink, proj, proj, proj,
      cache_k.reshape(DEC_BATCH, DEPTH, PAST_LEN, KV_WIDTH),
      cache_v.reshape(DEC_BATCH, DEPTH, PAST_LEN, KV_WIDTH), cos_t, sin_t)


def _rope_tables():
    half = HEAD_DIM // 2
    n_freq = half // 2
    t = jnp.arange(DEC_SEQ)
    row = (t // GRID_W).astype(F32)
    col = (t % GRID_W).astype(F32)
    freqs = ROPE_THETA ** (-jnp.arange(n_freq, dtype=F32) / n_freq)
    tabs = []
    for pos in (row, col):
        ang = pos[:, None] * freqs[None, :]
        tabs.append((jnp.cos(ang), jnp.sin(ang)))
    cos_t = jnp.concatenate([tabs[0][0], tabs[0][0], tabs[1][0], tabs[1][0]], axis=-1)
    sin_t = jnp.concatenate([-tabs[0][1], tabs[0][1], -tabs[1][1], tabs[1][1]], axis=-1)
    return cos_t, sin_t


def _log_sigmoid(x):
    return jnp.minimum(x, 0.0) - jnp.log1p(jnp.exp(-jnp.abs(x)))


def _gla_kernel(*refs, seq, n_heads, zero_init, unroll):
    n_in = 10 if zero_init else 12
    q_ref, k_ref, v_ref, og_ref, lr_ref, wgf_ref, wgb_ref, bgf_ref, bgb_ref, nw_ref = refs[:10]
    o_ref, sf_ref, sb_ref = refs[n_in:n_in + 3]
    (gf_scr, gb_scr, qef_scr, qeb_scr, o_scr, uf_scr, ub_scr, sinf_scr, sinb_scr,
     decf_scr, decb_scr, stf_scr, stb_scr) = refs[n_in + 3:]
    c = GLA_CHUNK
    grp = GLA_GROUP
    cpg = grp // c
    n_chunks = seq // c
    n_groups = seq // grp
    q_scale = GLA_DK ** -0.5

    lr = lr_ref[...].astype(BF16)
    gf_scr[...] = _log_sigmoid(_dot(lr, wgf_ref[...]) + bgf_ref[...]) / GATE_NORM
    gb_scr[...] = _log_sigmoid(_dot(lr, wgb_ref[...]) + bgb_ref[...]) / GATE_NORM

    ri = lax.broadcasted_iota(jnp.int32, (grp, grp), 0)
    ci = lax.broadcasted_iota(jnp.int32, (grp, grp), 1)
    same_chunk = (ri // c) == (ci // c)
    keep_f = same_chunk & (ri >= ci)
    keep_b = same_chunk & (ri <= ci)

    def cumsum(keep, g):
        tri = keep.astype(BF16)
        g_hi = g.astype(BF16)
        r1 = g - g_hi.astype(F32)
        g_mid = r1.astype(BF16)
        g_lo = (r1 - g_mid.astype(F32)).astype(BF16)
        return _dot(jnp.concatenate([tri, tri, tri], axis=1),
                    jnp.concatenate([g_hi, g_mid, g_lo], axis=0))

    dirs = ((gf_scr, keep_f, c - 1, qef_scr, uf_scr, decf_scr),
            (gb_scr, keep_b, 0, qeb_scr, ub_scr, decb_scr))

    def group_body(r, carry):
        rows = pl.ds(pl.multiple_of(r * grp, grp), grp)
        for h in range(n_heads):
            hk = slice(h * GLA_DK, (h + 1) * GLA_DK)
            hv = slice(h * GLA_DV, (h + 1) * GLA_DV)
            q = q_ref[rows, hk] * q_scale
            k = k_ref[rows, hk]
            v = v_ref[rows, hv].astype(BF16)
            a_sum = None
            for g_scr, keep, edge, qe_scr, u_scr, dec_scr in dirs:
                b = cumsum(keep, g_scr[rows, hk])
                qe = (q * jnp.exp(b)).astype(BF16)
                ke = k * jnp.exp(-b)
                qe_scr[rows, hk] = qe
                a = jnp.where(keep, _dot_nt(qe, ke.astype(BF16)), 0.0)
                a_sum = a if a_sum is None else a_sum + a
                for cc in range(cpg):
                    cr = slice(cc * c, (cc + 1) * c)
                    decay = jnp.exp(b[cc * c + edge:cc * c + edge + 1, :])
                    kd = (ke[cr] * decay).astype(BF16)
                    n = r * cpg + cc
                    u_scr[h, n] = _dot_tn(v[cr], kd)
                    dec_scr[h, n] = jnp.broadcast_to(decay, (8, GLA_DK))
            o_scr[rows, hv] = _dot(a_sum.astype(BF16), v)
        return carry

    lax.fori_loop(0, n_groups, group_body, 0, unroll=unroll)

    for h in range(n_heads):
        if zero_init:
            stf_scr[h] = jnp.zeros((GLA_DV, GLA_DK), F32)
            stb_scr[h] = jnp.zeros((GLA_DV, GLA_DK), F32)
        else:
            stf_scr[h] = refs[10][0, h].T
            stb_scr[h] = refs[11][0, h].T

    def scan_body(n, carry):
        nb = n_chunks - 1 - n
        for h in range(n_heads):
            s = stf_scr[h]
            sinf_scr[h, n] = s.astype(BF16)
            stf_scr[h] = decf_scr[h, n][0:1, :] * s + uf_scr[h, n]
            s = stb_scr[h]
            sinb_scr[h, nb] = s.astype(BF16)
            stb_scr[h] = decb_scr[h, nb][0:1, :] * s + ub_scr[h, nb]
        return carry

    lax.fori_loop(0, n_chunks, scan_body, 0)
    for h in range(n_heads):
        sf_ref[0, h] = stf_scr[h].T
        sb_ref[0, h] = stb_scr[h].T

    nw = nw_ref[...]

    def finish(r, carry):
        rows = pl.ds(pl.multiple_of(r * grp, grp), grp)
        for h in range(n_heads):
            hk = slice(h * GLA_DK, (h + 1) * GLA_DK)
            hv = slice(h * GLA_DV, (h + 1) * GLA_DV)
            parts = []
            for cc in range(cpg):
                n = r * cpg + cc
                cr = pl.ds(pl.multiple_of(r * grp + cc * c, c), c)
                qe2 = jnp.concatenate([qef_scr[cr, hk], qeb_scr[cr, hk]], axis=1)
                s2 = jnp.concatenate([sinf_scr[h, n], sinb_scr[h, n]], axis=1)
                parts.append(_dot_nt(qe2, s2))
            o = o_scr[rows, hv] + jnp.concatenate(parts, axis=0)
            o = o * lax.rsqrt(jnp.mean(o * o, axis=-1, keepdims=True) + LN_EPS) * nw
            o_ref[rows, hv] = (o * _silu(og_ref[rows, hv])).astype(BF16)
        return carry

    lax.fori_loop(0, n_groups, finish, 0, unroll=unroll)


def _gla(proj, proj_lr, wgf, wgb, bgf, bgb, norm_w, s0f, s0b, *, seq, n_batch, row0, n_heads):
    zero_init = s0f is None
    n_chunks = seq // GLA_CHUNK
    n_groups = seq // GLA_GROUP

    def col_spec(width, col):
        w = n_heads * width
        return pl.BlockSpec((seq, w), lambda b, h: (row0 + b, col // w + h))

    head_w = pl.BlockSpec((LR_PAD, n_heads * GLA_DK), lambda b, h: (0, h))
    head_b = pl.BlockSpec((1, n_heads * GLA_DK), lambda b, h: (0, h))
    state_spec = pl.BlockSpec((1, n_heads, GLA_DK, GLA_DV), lambda b, h: (b, h, 0, 0))
    in_specs = [
        col_spec(GLA_DK, COL_GQ), col_spec(GLA_DK, COL_GK), col_spec(GLA_DV, COL_GV),
        col_spec(GLA_DV, COL_GOG),
        pl.BlockSpec((seq, LR_PAD), lambda b, h: (row0 + b, 0)),
        head_w, head_w, head_b, head_b,
        pl.BlockSpec((1, GLA_DV), lambda b, h: (0, 0)),
    ]
    args = [proj, proj, proj, proj, proj_lr, wgf, wgb, bgf, bgb, norm_w]
    if not zero_init:
        in_specs += [state_spec, state_spec]
        args += [s0f, s0b]
    state_shape = jax.ShapeDtypeStruct((n_batch, GLA_HEADS, GLA_DK, GLA_DV), F32)
    per_chunk = (n_heads, n_chunks, GLA_DV, GLA_DK)
    return pl.pallas_call(
        functools.partial(_gla_kernel, seq=seq, n_heads=n_heads, zero_init=zero_init,
                          unroll=2 if n_groups > 1 else 1),
        out_shape=(jax.ShapeDtypeStruct((n_batch * seq, GLA_WIDTH), BF16), state_shape, state_shape),
        grid=(n_batch, GLA_HEADS // n_heads),
        in_specs=in_specs,
        out_specs=(pl.BlockSpec((seq, n_heads * GLA_DV), lambda b, h: (b, h)), state_spec, state_spec),
        scratch_shapes=[
            pltpu.VMEM((seq, n_heads * GLA_DK), F32), pltpu.VMEM((seq, n_heads * GLA_DK), F32),
            pltpu.VMEM((seq, n_heads * GLA_DK), BF16), pltpu.VMEM((seq, n_heads * GLA_DK), BF16),
            pltpu.VMEM((seq, n_heads * GLA_DV), F32),
            pltpu.VMEM(per_chunk, F32), pltpu.VMEM(per_chunk, F32),
            pltpu.VMEM(per_chunk, BF16), pltpu.VMEM(per_chunk, BF16),
            pltpu.VMEM((n_heads, n_chunks, 8, GLA_DK), F32),
            pltpu.VMEM((n_heads, n_chunks, 8, GLA_DK), F32),
            pltpu.VMEM((n_heads, GLA_DV, GLA_DK), F32), pltpu.VMEM((n_heads, GLA_DV, GLA_DK), F32),
        ],
        compiler_params=_params(("arbitrary", "arbitrary")),
        name="gla_ctx" if zero_init else "gla_lat",
    )(*args)


def _outproj_kernel(oa_ref, og_ref, x_ref, g_ref, w_ref, lw_ref, lb_ref, o_ref):
    f = _dot(oa_ref[...], w_ref[:ATTN_WIDTH, :]) + _dot(og_ref[...], w_ref[ATTN_WIDTH:, :])
    y = DEEPNORM_ALPHA * x_ref[...] + g_ref[0] * f
    o_ref[...] = _layer_norm(y, lw_ref[...], lb_ref[...])


def _out_proj(o_att, o_gla, x, mods, w_out, ln_w, ln_b, layer):
    tm = TM_OUT
    vec = pl.BlockSpec((1, D_MODEL), lambda i: (0, 0))
    return pl.pallas_call(
        _outproj_kernel,
        out_shape=jax.ShapeDtypeStruct((N_TOK, D_MODEL), F32),
        grid=(N_TOK // tm,),
        in_specs=[
            pl.BlockSpec((tm, ATTN_WIDTH), lambda i: (i, 0)),
            pl.BlockSpec((tm, GLA_WIDTH), lambda i: (i, 0)),
            pl.BlockSpec((tm, D_MODEL), lambda i: (i, 0)),
            _mod_spec(layer, 2, tm),
            pl.BlockSpec((D_MODEL, D_MODEL), lambda i: (0, 0)),
            vec, vec,
        ],
        out_specs=pl.BlockSpec((tm, D_MODEL), lambda i: (i, 0)),
        compiler_params=_params(("arbitrary",)),
        name="out_proj",
    )(o_att, o_gla, x, mods, w_out, ln_w, ln_b)


def _ffn_kernel(x_ref, xp_ref, xn_ref, sh_ref, sc_ref, g_ref, wa_ref, wg_ref, cwa_ref, cwg_ref,
                cba_ref, cbg_ref, wd_ref, lw_ref, lb_ref, o_ref, h_scr,
                ua0_scr, ug0_scr, ua1_scr, ug1_scr, act0_scr, act1_scr, *, tm, nf):
    i = pl.program_id(0)
    j = pl.program_id(1)
    rows = tm + 2 * HALO
    u_slots = ((ua0_scr, ug0_scr), (ua1_scr, ug1_scr))
    act_slots = (act0_scr, act1_scr)

    tf = wa_ref.shape[1]
    n_pieces = 4
    act_w = tf // n_pieces
    down_w = D_MODEL // n_pieces

    def up_piece(slot, p):
        which, half = divmod(p, 2)
        w_ref = (wa_ref, wg_ref)[which]
        cols = slice(half * (tf // 2), (half + 1) * (tf // 2))
        u_slots[slot][which][:, cols] = _dot(h_scr[...], w_ref[:, cols])

    def down_piece(slot, p):
        cols = slice(p * down_w, (p + 1) * down_w)
        o_ref[:, cols] += _dot(act_slots[slot][...], wd_ref[:, cols])

    def act_piece(slot, p):
        cols = slice(p * act_w, (p + 1) * act_w)
        seq_mask = jnp.where(i < N_CTX // tm, SEQ - 1, DEC_SEQ - 1)
        tr = 128
        for r0 in range(0, tm, tr):
            pos = (i * tm + r0 + lax.broadcasted_iota(jnp.int32, (tr, 1), 0)) & seq_mask
            first = pos == 0
            last = pos == seq_mask

            def conv(u_ref, cw_ref, cb_ref):
                cw = cw_ref[:, cols]
                prev = u_ref[HALO - 1 + r0:HALO - 1 + r0 + tr, cols]
                cur = u_ref[HALO + r0:HALO + r0 + tr, cols]
                nxt = u_ref[HALO + 1 + r0:HALO + 1 + r0 + tr, cols]
                return (jnp.where(first, 0.0, prev) * cw[0:1] + cur * cw[1:2]
                        + jnp.where(last, 0.0, nxt) * cw[2:3] + cb_ref[:, cols])

            a = conv(u_slots[slot][0], cwa_ref, cba_ref)
            g = conv(u_slots[slot][1], cwg_ref, cbg_ref)
            act_slots[slot][r0:r0 + tr, cols] = (_silu(g) * a).astype(BF16)

    def run(up=None, act=None, down=None):
        for p in range(n_pieces):
            if up is not None:
                up_piece(up, p)
            if act is not None:
                act_piece(act, p)
            if down is not None:
                down_piece(down, p)

    @pl.when(j == 0)
    def _():
        sc = 1.0 + sc_ref[0]
        sh = sh_ref[0]
        h_scr[0:HALO, :] = (xp_ref[...] * sc + sh).astype(BF16)
        h_scr[HALO:HALO + tm, :] = (x_ref[...] * sc + sh).astype(BF16)
        h_scr[HALO + tm:rows, :] = (xn_ref[...] * sc + sh).astype(BF16)
        o_ref[...] = jnp.zeros_like(o_ref)
        run(up=0)

    @pl.when(j == 1)
    def _():
        run(up=1, act=0)

    for parity in (0, 1):
        @pl.when((j >= 2) & (j < nf) & (j % 2 == parity))
        def _(parity=parity):
            run(up=parity, act=1 - parity, down=parity)

    @pl.when(j == nf)
    def _():
        run(act=(nf - 1) % 2, down=nf % 2)

    @pl.when(j == nf + 1)
    def _():
        run(down=(nf - 1) % 2)
        y = DEEPNORM_ALPHA * x_ref[...] + g_ref[0] * o_ref[...]
        o_ref[...] = _layer_norm(y, lw_ref[...], lb_ref[...])


def _ffn(x, mods, w_up, conv_w, conv_b, w_down, ln_w, ln_b, layer):
    tm, tf = TM_FFN, TF_FFN
    nf = D_FF // tf
    hb = tm // HALO
    n_halo_blocks = N_TOK // HALO
    vec = pl.BlockSpec((1, D_MODEL), lambda i, j: (0, 0))

    def chunk(j, lag):
        return jnp.clip(j - lag, 0, nf - 1)

    u_scratch = pltpu.VMEM((tm + 2 * HALO, tf), F32)
    act_scratch = pltpu.VMEM((tm, tf), BF16)
    return pl.pallas_call(
        functools.partial(_ffn_kernel, tm=tm, nf=nf),
        out_shape=jax.ShapeDtypeStruct((N_TOK, D_MODEL), F32),
        grid=(N_TOK // tm, nf + 2),
        in_specs=[
            pl.BlockSpec((tm, D_MODEL), lambda i, j: (i, 0)),
            pl.BlockSpec((HALO, D_MODEL), lambda i, j: (jnp.maximum(i * hb - 1, 0), 0)),
            pl.BlockSpec((HALO, D_MODEL),
                         lambda i, j: (jnp.minimum((i + 1) * hb, n_halo_blocks - 1), 0)),
            _mod_spec(layer, 3, tm), _mod_spec(layer, 4, tm), _mod_spec(layer, 5, tm),
            pl.BlockSpec((D_MODEL, tf), lambda i, j: (0, chunk(j, 0))),
            pl.BlockSpec((D_MODEL, tf), lambda i, j: (0, nf + chunk(j, 0))),
            pl.BlockSpec((3, tf), lambda i, j: (0, chunk(j, 1))),
            pl.BlockSpec((3, tf), lambda i, j: (0, nf + chunk(j, 1))),
            pl.BlockSpec((1, tf), lambda i, j: (0, chunk(j, 1))),
            pl.BlockSpec((1, tf), lambda i, j: (0, nf + chunk(j, 1))),
            pl.BlockSpec((tf, D_MODEL), lambda i, j: (chunk(j, 2), 0)),
            vec, vec,
        ],
        out_specs=pl.BlockSpec((tm, D_MODEL), lambda i, j: (i, 0)),
        scratch_shapes=[pltpu.VMEM((tm + 2 * HALO, D_MODEL), BF16),
                        u_scratch, u_scratch, u_scratch, u_scratch, act_scratch, act_scratch],
        compiler_params=_params(("arbitrary", "arbitrary")),
        name="ffn",
    )(x, x, x, mods, mods, mods, w_up, w_up, conv_w, conv_w, conv_b, conv_b, w_down, ln_w, ln_b)


def _pad_gate(w, row0):
    return jnp.zeros((LR_PAD, GLA_KEY_WIDTH), F32).at[row0:row0 + GATE_RANK].set(w).astype(BF16)


def kernel(x_prompt, x_sample, cache_k, cache_v, state_gla_fwd, state_gla_bwd, c, c_ctx, w_ada,
           b_ada, w_in, attn_sink, w_gate_f, b_gate_f, w_gate_b, b_gate_b, gla_norm_w, w_out,
           ln1_w, ln1_b, w_up, conv_w, conv_b, w_down, ln2_w, ln2_b):
    x = jnp.concatenate([x_prompt.reshape(N_CTX, D_MODEL), x_sample.reshape(N_LAT, D_MODEL)], axis=0)
    cc = jnp.concatenate([c_ctx[None, :], c, jnp.zeros((MOD_ROWS - 1 - DEC_BATCH, D_MODEL), F32)], axis=0)
    mods = _ada(cc, w_ada, b_ada).reshape(DEPTH * MOD_ROWS * N_MODS, 1, D_MODEL)
    cos_t, sin_t = _rope_tables()

    new_k, new_v, new_sf, new_sb = [], [], [], []
    for l in range(DEPTH):
        w_main = jnp.concatenate([w_in[l, :, a:b] for a, b in W_IN_ORDER], axis=1).astype(BF16)
        w_lr = jnp.pad(w_in[l, :, PROJ_MAIN:], ((0, 0), (0, LR_PAD - 2 * GATE_RANK))).astype(BF16)
        proj, proj_lr = _in_proj(x, mods, w_main, w_lr, l)

        o_att = jnp.concatenate([
            _attn_ctx(proj, attn_sink[l]),
            _attn_lat(proj, attn_sink[l], cache_k, cache_v, cos_t, sin_t, l)], axis=0)

        wgf = _pad_gate(w_gate_f[l], 0)
        wgb = _pad_gate(w_gate_b[l], GATE_RANK)
        bgf = b_gate_f[l][None, :]
        bgb = b_gate_b[l][None, :]
        norm_w = gla_norm_w[l][None, :]
        og_ctx, sf, sb = _gla(proj, proj_lr, wgf, wgb, bgf, bgb, norm_w, None, None,
                              seq=SEQ, n_batch=BATCH, row0=0, n_heads=GLA_HEADS)
        og_lat, _, _ = _gla(proj, proj_lr, wgf, wgb, bgf, bgb, norm_w,
                            state_gla_fwd[:, l], state_gla_bwd[:, l],
                            seq=DEC_SEQ, n_batch=DEC_BATCH, row0=N_CTX // DEC_SEQ, n_heads=1)
        o_gla = jnp.concatenate([og_ctx, og_lat], axis=0)

        x = _out_proj(o_att, o_gla, x, mods, w_out[l].astype(BF16), ln1_w[l][None, :],
                      ln1_b[l][None, :], l)
        x = _ffn(x, mods, w_up[l].astype(BF16), conv_w[l], conv_b[l][None, :],
                 w_down[l].astype(BF16), ln2_w[l][None, :], ln2_b[l][None, :], l)

        new_k.append(proj[:N_CTX, COL_K:COL_K + KV_WIDTH].reshape(BATCH, SEQ, ATTN_KV_HEADS, HEAD_DIM))
        new_v.append(proj[:N_CTX, COL_V:COL_V + KV_WIDTH].reshape(BATCH, SEQ, ATTN_KV_HEADS, HEAD_DIM))
        new_sf.append(sf)
        new_sb.append(sb)

    return (x[:N_CTX].reshape(BATCH, SEQ, D_MODEL), x[N_CTX:].reshape(DEC_BATCH, DEC_SEQ, D_MODEL),
            jnp.stack(new_k, axis=1), jnp.stack(new_v, axis=1),
            jnp.stack(new_sf, axis=1), jnp.stack(new_sb, axis=1))
```

```python
import functools

import jax
import jax.numpy as jnp
from jax import lax
from jax.experimental import pallas as pl
from jax.experimental.pallas import tpu as pltpu

F32 = jnp.float32
BF16 = jnp.bfloat16

D_MODEL = 2048
BATCH = 32
SEQ = 256
DEPTH = 2
DEC_BATCH = 4
DEC_SEQ = 2048
PAST_LEN = 256
GRID_W = 64
HEAD_DIM = 128
ATTN_WIDTH = 1024
ATTN_HEADS = 8
ATTN_KV_HEADS = 2
ATTN_GROUP = 4
KV_WIDTH = 256
WINDOW = 128
BLOCK = 128
GLA_WIDTH = 1024
GLA_HEADS = 4
GLA_DV = 256
GLA_DK = 128
GLA_KEY_WIDTH = 512
GATE_RANK = 16
GATE_NORM = 16.0
GLA_CHUNK = 64
GLA_GROUP = 256
GLA_CHAINS = 8
D_FF = 5632
ROPE_THETA = 10000.0
LN_EPS = 1e-5
NEG_INF = -1e30
DEEPNORM_ALPHA = (2 * DEPTH) ** 0.25

N_CTX = BATCH * SEQ
N_LAT = DEC_BATCH * DEC_SEQ
MOD_ROWS = 8
N_MODS = 6

PROJ_MAIN = 4608
COL_GV = ATTN_WIDTH
COL_GOG = COL_GV + GLA_WIDTH
COL_GQ = COL_GOG + GLA_WIDTH
COL_GK = COL_GQ + GLA_KEY_WIDTH
COL_K = COL_GK + GLA_KEY_WIDTH
COL_V = COL_K + KV_WIDTH
LR_PAD = 128
W_IN_ORDER = ((0, 1024), (2560, 3584), (3584, 4608), (1536, 2048), (2048, 2560), (1024, 1280),
              (1280, 1536))

VMEM_LIMIT = 56 * 1024 * 1024

TM_PROJ = 1024
TN_PROJ = 1536
TM_OUT = 512
TM_FFN = 512
TF_FFN = 512
HALO = 16
TN_ADA = 1024


class Path:
    def __init__(self, name, n_batch, seq):
        self.name = name
        self.n_batch = n_batch
        self.seq = seq
        self.rows = n_batch * seq
        self.is_ctx = name == "ctx"

    def mod_row(self, i, tm):
        return 0 if self.is_ctx else 1 + i // (self.seq // tm)


CTX = Path("ctx", BATCH, SEQ)
LAT = Path("lat", DEC_BATCH, DEC_SEQ)


def _params(sem):
    return pltpu.CompilerParams(dimension_semantics=sem, vmem_limit_bytes=VMEM_LIMIT)


def _silu(x):
    return x / (1.0 + jnp.exp(-x))


def _layer_norm(y, w, b):
    mu = jnp.mean(y, axis=-1, keepdims=True)
    d = y - mu
    var = jnp.mean(d * d, axis=-1, keepdims=True)
    return d * lax.rsqrt(var + LN_EPS) * w + b


def _dot(a, b):
    return jnp.dot(a, b, preferred_element_type=F32)


def _dot_nt(a, b):
    return lax.dot_general(a, b, (((1,), (1,)), ((), ())), preferred_element_type=F32)


def _dot_tn(a, b):
    return lax.dot_general(a, b, (((0,), (0,)), ((), ())), preferred_element_type=F32)


def _mod_spec(path, layer, chunk, tm):
    def index_map(i, *_):
        return ((layer * MOD_ROWS + path.mod_row(i, tm)) * N_MODS + chunk, 0, 0)
    return pl.BlockSpec((1, 1, D_MODEL), index_map)


def _layer_vec_spec(layer, width, n_grid):
    if n_grid == 1:
        return pl.BlockSpec((1, 1, width), lambda i: (layer, 0, 0))
    return pl.BlockSpec((1, 1, width), lambda i, j: (layer, 0, 0))


def _ada_kernel(c_ref, w_ref, b_ref, o_ref):
    s = _silu(c_ref[...]).astype(BF16)
    o_ref[0] = _dot(s, w_ref[0].astype(BF16)) + b_ref[0]


def _ada(cc, w_ada, b_ada):
    n_out = N_MODS * D_MODEL
    return pl.pallas_call(
        _ada_kernel,
        out_shape=jax.ShapeDtypeStruct((DEPTH, MOD_ROWS, n_out), F32),
        grid=(DEPTH, n_out // TN_ADA),
        in_specs=[
            pl.BlockSpec((MOD_ROWS, D_MODEL), lambda l, j: (0, 0)),
            pl.BlockSpec((1, D_MODEL, TN_ADA), lambda l, j: (l, 0, j)),
            pl.BlockSpec((1, 1, TN_ADA), lambda l, j: (l, 0, j)),
        ],
        out_specs=pl.BlockSpec((1, MOD_ROWS, TN_ADA), lambda l, j: (l, 0, j)),
        compiler_params=_params(("arbitrary", "arbitrary")),
        name="ada",
    )(cc, w_ada, b_ada.reshape(DEPTH, 1, n_out))


def _inproj_kernel(x_ref, sh_ref, sc_ref, w_ref, wlr_ref, o_ref, olr_ref, h_scr):
    @pl.when(pl.program_id(1) == 0)
    def _():
        h = (x_ref[...] * (1.0 + sc_ref[0]) + sh_ref[0]).astype(BF16)
        h_scr[...] = h
        olr_ref[...] = _dot(h, wlr_ref[0])
    o_ref[...] = _dot(h_scr[...], w_ref[0, 0])


def _in_proj(path, x, mods, w_main, w_lr, layer):
    tm, tn = TM_PROJ, TN_PROJ
    return pl.pallas_call(
        _inproj_kernel,
        out_shape=(jax.ShapeDtypeStruct((path.rows, PROJ_MAIN), F32),
                   jax.ShapeDtypeStruct((path.rows, LR_PAD), F32)),
        grid=(path.rows // tm, PROJ_MAIN // tn),
        in_specs=[
            pl.BlockSpec((tm, D_MODEL), lambda i, j: (i, 0)),
            _mod_spec(path, layer, 0, tm),
            _mod_spec(path, layer, 1, tm),
            pl.BlockSpec((1, 1, D_MODEL, tn), lambda i, j: (layer, j, 0, 0)),
            pl.BlockSpec((1, D_MODEL, LR_PAD), lambda i, j: (layer, 0, 0)),
        ],
        out_specs=(pl.BlockSpec((tm, tn), lambda i, j: (i, j)),
                   pl.BlockSpec((tm, LR_PAD), lambda i, j: (i, 0))),
        scratch_shapes=[pltpu.VMEM((tm, D_MODEL), BF16)],
        compiler_params=_params(("arbitrary", "arbitrary")),
        name="in_proj_" + path.name,
    )(x, mods, mods, w_main, w_lr)


def _sink_softmax_pv(sinks, scores, values):
    heads = range(len(sinks))
    ms = [functools.reduce(jnp.maximum, [jnp.max(s, axis=-1, keepdims=True) for s in scores[h]],
                           sinks[h]) for h in heads]
    es = [[jnp.exp(s - ms[h]) for s in scores[h]] for h in heads]
    dens = [functools.reduce(lambda a, e: a + jnp.sum(e, axis=-1, keepdims=True), es[h],
                             jnp.exp(sinks[h] - ms[h])) for h in heads]
    invs = [1.0 / d for d in dens]
    ps = [[(e * invs[h]).astype(BF16) for e in es[h]] for h in heads]
    return [functools.reduce(lambda a, b: a + b, [_dot(p, v) for p, v in zip(ps[h], values[h])])
            for h in heads]


def _attn_ctx_kernel(sink_ref, q_ref, k_ref, v_ref, o_ref, *, layer):
    scale = HEAD_DIM ** -0.5
    heads = range(ATTN_HEADS)
    ks = [k_ref[:, kh * HEAD_DIM:(kh + 1) * HEAD_DIM].astype(BF16) for kh in range(ATTN_KV_HEADS)]
    vs = [v_ref[:, kh * HEAD_DIM:(kh + 1) * HEAD_DIM].astype(BF16) for kh in range(ATTN_KV_HEADS)]
    scores = [[_dot_nt(q_ref[:, h * HEAD_DIM:(h + 1) * HEAD_DIM].astype(BF16),
                       ks[h // ATTN_GROUP]) * scale] for h in heads]
    outs = _sink_softmax_pv([sink_ref[layer, h] for h in heads], scores,
                            [[vs[h // ATTN_GROUP]] for h in heads])
    for h in heads:
        o_ref[:, h * HEAD_DIM:(h + 1) * HEAD_DIM] = outs[h].astype(BF16)


def _attn_ctx(proj, sink, layer):
    return pl.pallas_call(
        functools.partial(_attn_ctx_kernel, layer=layer),
        out_shape=jax.ShapeDtypeStruct((N_CTX, ATTN_WIDTH), BF16),
        grid=(BATCH,),
        in_specs=[
            pl.BlockSpec(memory_space=pltpu.SMEM),
            pl.BlockSpec((SEQ, ATTN_WIDTH), lambda b: (b, 0)),
            pl.BlockSpec((SEQ, KV_WIDTH), lambda b: (b, COL_K // KV_WIDTH)),
            pl.BlockSpec((SEQ, KV_WIDTH), lambda b: (b, COL_V // KV_WIDTH)),
        ],
        out_specs=pl.BlockSpec((SEQ, ATTN_WIDTH), lambda b: (b, 0)),
        compiler_params=_params(("arbitrary",)),
        name="attn_ctx",
    )(sink, proj, proj, proj)


def _attn_lat_kernel(sink_ref, q_ref, k_ref, v_ref, ck_ref, cv_ref, cos_ref, sin_ref, o_ref, *,
                     layer):
    scale = HEAD_DIM ** -0.5
    win = 3 * BLOCK
    i = pl.program_id(1)
    q0 = pl.multiple_of(i * BLOCK, BLOCK)
    start = pl.multiple_of(jnp.clip((i - 1) * BLOCK, 0, DEC_SEQ - win), BLOCK)
    cos_q, sin_q = cos_ref[pl.ds(q0, BLOCK), :], sin_ref[pl.ds(q0, BLOCK), :]
    cos_k, sin_k = cos_ref[pl.ds(start, win), :], sin_ref[pl.ds(start, win), :]
    lane = lax.broadcasted_iota(jnp.int32, (1, HEAD_DIM), 1)
    low = (lane % (HEAD_DIM // 2)) < (HEAD_DIM // 4)

    def rope(x, c, s):
        partner = jnp.where(low, pltpu.roll(x, HEAD_DIM - HEAD_DIM // 4, 1),
                            pltpu.roll(x, HEAD_DIM // 4, 1))
        return x * c + partner * s

    qpos = q0 + lax.broadcasted_iota(jnp.int32, (BLOCK, 1), 0)
    kpos = start + lax.broadcasted_iota(jnp.int32, (1, win), 1)
    valid = jnp.abs(qpos - kpos) <= WINDOW

    heads = range(ATTN_HEADS)
    kv_cols = [slice(kh * HEAD_DIM, (kh + 1) * HEAD_DIM) for kh in range(ATTN_KV_HEADS)]
    kws = [rope(k_ref[pl.ds(start, win), c], cos_k, sin_k).astype(BF16) for c in kv_cols]
    vws = [v_ref[pl.ds(start, win), c].astype(BF16) for c in kv_cols]
    cks = [ck_ref[0, 0, :, c].astype(BF16) for c in kv_cols]
    cvs = [cv_ref[0, 0, :, c].astype(BF16) for c in kv_cols]
    qs = [rope(q_ref[:, h * HEAD_DIM:(h + 1) * HEAD_DIM], cos_q, sin_q).astype(BF16) for h in heads]
    scores = [[jnp.where(valid, _dot_nt(qs[h], kws[h // ATTN_GROUP]) * scale, NEG_INF),
               _dot_nt(qs[h], cks[h // ATTN_GROUP]) * scale] for h in heads]
    outs = _sink_softmax_pv([sink_ref[layer, h] for h in heads], scores,
                            [[vws[h // ATTN_GROUP], cvs[h // ATTN_GROUP]] for h in heads])
    for h in heads:
        o_ref[:, h * HEAD_DIM:(h + 1) * HEAD_DIM] = outs[h].astype(BF16)


def _attn_lat(proj, sink, cache_k, cache_v, cos_t, sin_t, layer):
    nb = DEC_SEQ // BLOCK
    cache_spec = pl.BlockSpec((1, 1, PAST_LEN, KV_WIDTH), lambda b, i: (b, layer, 0, 0))
    table_spec = pl.BlockSpec((DEC_SEQ, HEAD_DIM), lambda b, i: (0, 0))
    return pl.pallas_call(
        functools.partial(_attn_lat_kernel, layer=layer),
        out_shape=jax.ShapeDtypeStruct((N_LAT, ATTN_WIDTH), BF16),
        grid=(DEC_BATCH, nb),
        in_specs=[
            pl.BlockSpec(memory_space=pltpu.SMEM),
            pl.BlockSpec((BLOCK, ATTN_WIDTH), lambda b, i: (b * nb + i, 0)),
            pl.BlockSpec((DEC_SEQ, KV_WIDTH), lambda b, i: (b, COL_K // KV_WIDTH)),
            pl.BlockSpec((DEC_SEQ, KV_WIDTH), lambda b, i: (b, COL_V // KV_WIDTH)),
            cache_spec, cache_spec, table_spec, table_spec,
        ],
        out_specs=pl.BlockSpec((BLOCK, ATTN_WIDTH), lambda b, i: (b * nb + i, 0)),
        compiler_params=_params(("arbitrary", "arbitrary")),
        name="attn_lat",
    )(sink, proj, proj, proj,
      cache_k.reshape(DEC_BATCH, DEPTH, PAST_LEN, KV_WIDTH),
      cache_v.reshape(DEC_BATCH, DEPTH, PAST_LEN, KV_WIDTH), cos_t, sin_t)


def _rope_tables():
    half = HEAD_DIM // 2
    n_freq = half // 2
    t = jnp.arange(DEC_SEQ)
    row = (t // GRID_W).astype(F32)
    col = (t % GRID_W).astype(F32)
    freqs = ROPE_THETA ** (-jnp.arange(n_freq, dtype=F32) / n_freq)
    tabs = []
    for pos in (row, col):
        ang = pos[:, None] * freqs[None, :]
        tabs.append((jnp.cos(ang), jnp.sin(ang)))
    cos_t = jnp.concatenate([tabs[0][0], tabs[0][0], tabs[1][0], tabs[1][0]], axis=-1)
    sin_t = jnp.concatenate([-tabs[0][1], tabs[0][1], -tabs[1][1], tabs[1][1]], axis=-1)
    return cos_t, sin_t


def _log_sigmoid(x):
    return jnp.minimum(x, 0.0) - jnp.log1p(jnp.exp(-jnp.abs(x)))


def _gla_kernel(*refs, seq, n_heads, is_ctx, gpi):
    q_ref, k_ref, v_ref, og_ref, lr_ref, wgf_ref, wgb_ref, bgf_ref, bgb_ref, nw_ref = refs[:10]
    if is_ctx:
        o_ref, sf_ref, sb_ref = refs[-16:-13]
    else:
        s0f_ref, s0b_ref = refs[10:12]
        o_ref = refs[-14]
    (gf_scr, gb_scr, qef_scr, qeb_scr, o_scr, uf_scr, ub_scr, sinf_scr, sinb_scr,
     decf_scr, decb_scr, stf_scr, stb_scr) = refs[-13:]
    c = GLA_CHUNK
    grp = GLA_GROUP
    cpg = grp // c
    n_chunks = seq // c
    n_groups = seq // grp
    q_scale = GLA_DK ** -0.5

    lr = lr_ref[...].astype(BF16)
    gf_scr[...] = _log_sigmoid(_dot(lr, wgf_ref[0]) + bgf_ref[0]) / GATE_NORM
    gb_scr[...] = _log_sigmoid(_dot(lr, wgb_ref[0]) + bgb_ref[0]) / GATE_NORM

    ri = lax.broadcasted_iota(jnp.int32, (grp, grp), 0)
    ci = lax.broadcasted_iota(jnp.int32, (grp, grp), 1)
    same_chunk = (ri // c) == (ci // c)
    keep_f = same_chunk & (ri >= ci)
    keep_b = same_chunk & (ri <= ci)

    def cumsum(keep, g):
        tri = keep.astype(BF16)
        g_hi = g.astype(BF16)
        r1 = g - g_hi.astype(F32)
        g_mid = r1.astype(BF16)
        g_lo = (r1 - g_mid.astype(F32)).astype(BF16)
        return _dot(jnp.concatenate([tri, tri, tri], axis=1),
                    jnp.concatenate([g_hi, g_mid, g_lo], axis=0))

    dirs = ((gf_scr, keep_f, c - 1, qef_scr, uf_scr, decf_scr),
            (gb_scr, keep_b, 0, qeb_scr, ub_scr, decb_scr))

    hks = [slice(h * GLA_DK, (h + 1) * GLA_DK) for h in range(n_heads)]
    hvs = [slice(h * GLA_DV, (h + 1) * GLA_DV) for h in range(n_heads)]

    def group_body(it, carry):
        items = [(it * gpi + gi, h) for gi in range(gpi) for h in range(n_heads)]
        rows = [pl.ds(pl.multiple_of(r * grp, grp), grp) for r, _ in items]
        chains = [(x, d) for x in range(len(items)) for d in range(2)]
        qs = [q_ref[rows[x], hks[h]] * q_scale for x, (_, h) in enumerate(items)]
        ks = [k_ref[rows[x], hks[h]] for x, (_, h) in enumerate(items)]
        vs = [v_ref[rows[x], hvs[h]].astype(BF16) for x, (_, h) in enumerate(items)]
        bs = [cumsum(dirs[d][1], dirs[d][0][rows[x], hks[items[x][1]]]) for x, d in chains]
        ebs = [jnp.exp(b) for b in bs]
        embs = [jnp.exp(-b) for b in bs]
        qes = [(qs[x] * ebs[i]).astype(BF16) for i, (x, d) in enumerate(chains)]
        kes = [ks[x] * embs[i] for i, (x, d) in enumerate(chains)]
        for i, (x, d) in enumerate(chains):
            dirs[d][3][rows[x], hks[items[x][1]]] = qes[i]
        avals = [jnp.where(dirs[d][1], _dot_nt(qes[i], kes[i].astype(BF16)), 0.0)
                 for i, (x, d) in enumerate(chains)]
        decays = [[jnp.exp(bs[i][cc * c + dirs[d][2]:cc * c + dirs[d][2] + 1, :])
                   for cc in range(cpg)] for i, (x, d) in enumerate(chains)]
        kds = [[(kes[i][cc * c:(cc + 1) * c] * decays[i][cc]).astype(BF16) for cc in range(cpg)]
               for i in range(len(chains))]
        for i, (x, d) in enumerate(chains):
            r, h = items[x]
            for cc in range(cpg):
                n = r * cpg + cc
                dirs[d][4][h, n] = _dot_tn(vs[x][cc * c:(cc + 1) * c], kds[i][cc])
                dirs[d][5][h, n] = jnp.broadcast_to(decays[i][cc], (8, GLA_DK))
        for x, (_, h) in enumerate(items):
            a_sum = (avals[2 * x] + avals[2 * x + 1]).astype(BF16)
            o_scr[rows[x], hvs[h]] = _dot(a_sum, vs[x])
        return carry

    lax.fori_loop(0, n_groups // gpi, group_body, 0)

    for h in range(n_heads):
        if is_ctx:
            stf_scr[h] = jnp.zeros((GLA_DV, GLA_DK), F32)
            stb_scr[h] = jnp.zeros((GLA_DV, GLA_DK), F32)
        else:
            stf_scr[h] = s0f_ref[0, 0, h].T
            stb_scr[h] = s0b_ref[0, 0, h].T

    def scan_body(n, carry):
        nb = n_chunks - 1 - n
        for h in range(n_heads):
            s = stf_scr[h]
            sinf_scr[h, n] = s.astype(BF16)
            stf_scr[h] = decf_scr[h, n][0:1, :] * s + uf_scr[h, n]
            s = stb_scr[h]
            sinb_scr[h, nb] = s.astype(BF16)
            stb_scr[h] = decb_scr[h, nb][0:1, :] * s + ub_scr[h, nb]
        return carry

    lax.fori_loop(0, n_chunks, scan_body, 0)
    if is_ctx:
        for h in range(n_heads):
            sf_ref[0, 0, h] = stf_scr[h].T
            sb_ref[0, 0, h] = stb_scr[h].T

    nw = nw_ref[0]

    def finish(it, carry):
        items = [(it * gpi + gi, h) for gi in range(gpi) for h in range(n_heads)]
        rows = [pl.ds(pl.multiple_of(r * grp, grp), grp) for r, _ in items]
        inter = []
        for r, h in items:
            parts = []
            for cc in range(cpg):
                n = r * cpg + cc
                cr = pl.ds(pl.multiple_of(r * grp + cc * c, c), c)
                qe2 = jnp.concatenate([qef_scr[cr, hks[h]], qeb_scr[cr, hks[h]]], axis=1)
                s2 = jnp.concatenate([sinf_scr[h, n], sinb_scr[h, n]], axis=1)
                parts.append(_dot_nt(qe2, s2))
            inter.append(jnp.concatenate(parts, axis=0))
        os = [o_scr[rows[x], hvs[h]] + inter[x] for x, (_, h) in enumerate(items)]
        scales = [lax.rsqrt(jnp.mean(o * o, axis=-1, keepdims=True) + LN_EPS) for o in os]
        gates = [_silu(og_ref[rows[x], hvs[h]]) for x, (_, h) in enumerate(items)]
        for x, (_, h) in enumerate(items):
            o_ref[rows[x], hvs[h]] = (os[x] * scales[x] * nw * gates[x]).astype(BF16)
        return carry

    lax.fori_loop(0, n_groups // gpi, finish, 0)


def _gla(path, proj, proj_lr, wgf, wgb, bgf, bgb, norm_w, layer, *, s0=None, states=None):
    seq = path.seq
    n_heads = GLA_HEADS if path.is_ctx else 1
    n_chunks = seq // GLA_CHUNK
    n_groups = seq // GLA_GROUP

    def col_spec(width, col):
        w = n_heads * width
        return pl.BlockSpec((seq, w), lambda b, h: (b, col // w + h))

    head_w = pl.BlockSpec((1, LR_PAD, n_heads * GLA_DK), lambda b, h: (layer, 0, h))
    head_b = pl.BlockSpec((1, 1, n_heads * GLA_DK), lambda b, h: (layer, 0, h))
    state_spec = pl.BlockSpec((1, 1, n_heads, GLA_DK, GLA_DV), lambda b, h: (b, layer, h, 0, 0))
    in_specs = [
        col_spec(GLA_DK, COL_GQ), col_spec(GLA_DK, COL_GK), col_spec(GLA_DV, COL_GV),
        col_spec(GLA_DV, COL_GOG),
        pl.BlockSpec((seq, LR_PAD), lambda b, h: (b, 0)),
        head_w, head_w, head_b, head_b,
        pl.BlockSpec((1, 1, GLA_DV), lambda b, h: (layer, 0, 0)),
    ]
    args = [proj, proj, proj, proj, proj_lr, wgf, wgb, bgf, bgb, norm_w]
    o_shape = jax.ShapeDtypeStruct((path.rows, GLA_WIDTH), BF16)
    o_spec = pl.BlockSpec((seq, n_heads * GLA_DV), lambda b, h: (b, h))
    aliases = {}
    if path.is_ctx:
        state_shape = jax.ShapeDtypeStruct((BATCH, DEPTH, GLA_HEADS, GLA_DK, GLA_DV), F32)
        out_shape = (o_shape, state_shape, state_shape)
        out_specs = (o_spec, state_spec, state_spec)
        in_specs += [pl.BlockSpec(memory_space=pl.ANY), pl.BlockSpec(memory_space=pl.ANY)]
        args += list(states)
        aliases = {10: 1, 11: 2}
    else:
        in_specs += [state_spec, state_spec]
        args += list(s0)
        out_shape = (o_shape,)
        out_specs = (o_spec,)
    per_chunk = (n_heads, n_chunks, GLA_DV, GLA_DK)
    return pl.pallas_call(
        functools.partial(_gla_kernel, seq=seq, n_heads=n_heads, is_ctx=path.is_ctx,
                          gpi=min(n_groups, GLA_CHAINS // (2 * n_heads))),
        out_shape=out_shape,
        grid=(path.n_batch, GLA_HEADS // n_heads),
        in_specs=in_specs,
        out_specs=out_specs,
        input_output_aliases=aliases,
        scratch_shapes=[
            pltpu.VMEM((seq, n_heads * GLA_DK), F32), pltpu.VMEM((seq, n_heads * GLA_DK), F32),
            pltpu.VMEM((seq, n_heads * GLA_DK), BF16), pltpu.VMEM((seq, n_heads * GLA_DK), BF16),
            pltpu.VMEM((seq, n_heads * GLA_DV), F32),
            pltpu.VMEM(per_chunk, F32), pltpu.VMEM(per_chunk, F32),
            pltpu.VMEM(per_chunk, BF16), pltpu.VMEM(per_chunk, BF16),
            pltpu.VMEM((n_heads, n_chunks, 8, GLA_DK), F32),
            pltpu.VMEM((n_heads, n_chunks, 8, GLA_DK), F32),
            pltpu.VMEM((n_heads, GLA_DV, GLA_DK), F32), pltpu.VMEM((n_heads, GLA_DV, GLA_DK), F32),
        ],
        compiler_params=_params(("arbitrary", "arbitrary")),
        name="gla_" + path.name,
    )(*args)


def _outproj_kernel(oa_ref, og_ref, x_ref, g_ref, w_ref, lw_ref, lb_ref, o_ref):
    f = _dot(oa_ref[...], w_ref[0, :ATTN_WIDTH, :]) + _dot(og_ref[...], w_ref[0, ATTN_WIDTH:, :])
    y = DEEPNORM_ALPHA * x_ref[...] + g_ref[0] * f
    o_ref[...] = _layer_norm(y, lw_ref[0], lb_ref[0])


def _out_proj(path, o_att, o_gla, x, mods, w_out, ln_w, ln_b, layer):
    tm = TM_OUT
    vec = _layer_vec_spec(layer, D_MODEL, 1)
    return pl.pallas_call(
        _outproj_kernel,
        out_shape=jax.ShapeDtypeStruct((path.rows, D_MODEL), F32),
        grid=(path.rows // tm,),
        in_specs=[
            pl.BlockSpec((tm, ATTN_WIDTH), lambda i: (i, 0)),
            pl.BlockSpec((tm, GLA_WIDTH), lambda i: (i, 0)),
            pl.BlockSpec((tm, D_MODEL), lambda i: (i, 0)),
            _mod_spec(path, layer, 2, tm),
            pl.BlockSpec((1, D_MODEL, D_MODEL), lambda i: (layer, 0, 0)),
            vec, vec,
        ],
        out_specs=pl.BlockSpec((tm, D_MODEL), lambda i: (i, 0)),
        compiler_params=_params(("arbitrary",)),
        name="out_proj_" + path.name,
    )(o_att, o_gla, x, mods, w_out, ln_w, ln_b)


def _ffn_kernel(*refs, tm, nf, seq, has_halo):
    if has_halo:
        x_ref, xp_ref, xn_ref = refs[:3]
        refs = refs[3:]
    else:
        x_ref = refs[0]
        refs = refs[1:]
    (sh_ref, sc_ref, g_ref, wa_ref, wg_ref, cwa_ref, cwg_ref, cba_ref, cbg_ref, wd_ref, lw_ref,
     lb_ref, o_ref, h_scr, ua0_scr, ug0_scr, ua1_scr, ug1_scr, act0_scr, act1_scr) = refs
    i = pl.program_id(0)
    j = pl.program_id(1)
    hrows = HALO if has_halo else 0
    u0 = HALO - hrows
    u_slots = ((ua0_scr, ug0_scr), (ua1_scr, ug1_scr))
    act_slots = (act0_scr, act1_scr)

    tf = wa_ref.shape[-1]
    n_pieces = 4
    act_w = tf // n_pieces
    down_w = D_MODEL // n_pieces

    def up_piece(slot, p):
        which, half = divmod(p, 2)
        w_ref = (wa_ref, wg_ref)[which]
        cols = slice(half * (tf // 2), (half + 1) * (tf // 2))
        u_slots[slot][which][u0:u0 + tm + 2 * hrows, cols] = _dot(h_scr[...], w_ref[0, 0, 0, :, cols])

    def down_piece(slot, p):
        cols = slice(p * down_w, (p + 1) * down_w)
        o_ref[:, cols] += _dot(act_slots[slot][...], wd_ref[0, :, cols])

    def act_piece(slot, p):
        cols = slice(p * act_w, (p + 1) * act_w)
        tr = 128
        for r0 in range(0, tm, tr):
            pos = (i * tm + r0 + lax.broadcasted_iota(jnp.int32, (tr, 1), 0)) & (seq - 1)
            first = pos == 0
            last = pos == seq - 1

            def conv(u_ref, cw_ref, cb_ref):
                cw = cw_ref[0, :, cols]
                prev = u_ref[HALO - 1 + r0:HALO - 1 + r0 + tr, cols]
                cur = u_ref[HALO + r0:HALO + r0 + tr, cols]
                nxt = u_ref[HALO + 1 + r0:HALO + 1 + r0 + tr, cols]
                return (jnp.where(first, 0.0, prev) * cw[0:1] + cur * cw[1:2]
                        + jnp.where(last, 0.0, nxt) * cw[2:3] + cb_ref[0, :, cols])

            a = conv(u_slots[slot][0], cwa_ref, cba_ref)
            g = conv(u_slots[slot][1], cwg_ref, cbg_ref)
            act_slots[slot][r0:r0 + tr, cols] = (_silu(g) * a).astype(BF16)

    def run(up=None, act=None, down=None):
        for p in range(n_pieces):
            if up is not None:
                up_piece(up, p)
            if act is not None:
                act_piece(act, p)
            if down is not None:
                down_piece(down, p)

    @pl.when(j == 0)
    def _():
        sc = 1.0 + sc_ref[0]
        sh = sh_ref[0]
        h_scr[hrows:hrows + tm, :] = (x_ref[...] * sc + sh).astype(BF16)
        if has_halo:
            h_scr[0:HALO, :] = (xp_ref[...] * sc + sh).astype(BF16)
            h_scr[HALO + tm:2 * HALO + tm, :] = (xn_ref[...] * sc + sh).astype(BF16)
        else:
            for pair in u_slots:
                for u_scr in pair:
                    u_scr[0:HALO, :] = jnp.zeros((HALO, tf), F32)
                    u_scr[HALO + tm:2 * HALO + tm, :] = jnp.zeros((HALO, tf), F32)
        o_ref[...] = jnp.zeros_like(o_ref)
        run(up=0)

    @pl.when(j == 1)
    def _():
        run(up=1, act=0)

    for parity in (0, 1):
        @pl.when((j >= 2) & (j < nf) & (j % 2 == parity))
        def _(parity=parity):
            run(up=parity, act=1 - parity, down=parity)

    @pl.when(j == nf)
    def _():
        run(act=(nf - 1) % 2, down=nf % 2)

    @pl.when(j == nf + 1)
    def _():
        run(down=(nf - 1) % 2)
        y = DEEPNORM_ALPHA * x_ref[...] + g_ref[0] * o_ref[...]
        o_ref[...] = _layer_norm(y, lw_ref[0], lb_ref[0])


def _ffn(path, x, mods, w_up, conv_w, conv_b, w_down, ln_w, ln_b, layer):
    tm, tf = TM_FFN, TF_FFN
    nf = D_FF // tf
    hb = tm // HALO
    n_halo_blocks = path.rows // HALO
    has_halo = path.seq > tm
    vec = _layer_vec_spec(layer, D_MODEL, 2)

    def chunk(j, lag):
        return jnp.clip(j - lag, 0, nf - 1)

    x_specs = [pl.BlockSpec((tm, D_MODEL), lambda i, j: (i, 0))]
    x_args = [x]
    if has_halo:
        x_specs += [
            pl.BlockSpec((HALO, D_MODEL), lambda i, j: (jnp.maximum(i * hb - 1, 0), 0)),
            pl.BlockSpec((HALO, D_MODEL),
                         lambda i, j: (jnp.minimum((i + 1) * hb, n_halo_blocks - 1), 0)),
        ]
        x_args += [x, x]
    h_rows = tm + 2 * HALO if has_halo else tm
    u_scratch = pltpu.VMEM((tm + 2 * HALO, tf), F32)
    act_scratch = pltpu.VMEM((tm, tf), BF16)
    return pl.pallas_call(
        functools.partial(_ffn_kernel, tm=tm, nf=nf, seq=path.seq, has_halo=has_halo),
        out_shape=jax.ShapeDtypeStruct((path.rows, D_MODEL), F32),
        grid=(path.rows // tm, nf + 2),
        in_specs=x_specs + [
            _mod_spec(path, layer, 3, tm), _mod_spec(path, layer, 4, tm),
            _mod_spec(path, layer, 5, tm),
            pl.BlockSpec((1, 1, 1, D_MODEL, tf), lambda i, j: (layer, 0, chunk(j, 0), 0, 0)),
            pl.BlockSpec((1, 1, 1, D_MODEL, tf), lambda i, j: (layer, 1, chunk(j, 0), 0, 0)),
            pl.BlockSpec((1, 3, tf), lambda i, j: (layer, 0, chunk(j, 1))),
            pl.BlockSpec((1, 3, tf), lambda i, j: (layer, 0, nf + chunk(j, 1))),
            pl.BlockSpec((1, 1, tf), lambda i, j: (layer, 0, chunk(j, 1))),
            pl.BlockSpec((1, 1, tf), lambda i, j: (layer, 0, nf + chunk(j, 1))),
            pl.BlockSpec((1, tf, D_MODEL), lambda i, j: (layer, chunk(j, 2), 0)),
            vec, vec,
        ],
        out_specs=pl.BlockSpec((tm, D_MODEL), lambda i, j: (i, 0)),
        scratch_shapes=[pltpu.VMEM((h_rows, D_MODEL), BF16),
                        u_scratch, u_scratch, u_scratch, u_scratch, act_scratch, act_scratch],
        compiler_params=_params(("arbitrary", "arbitrary")),
        name="ffn_" + path.name,
    )(*x_args, mods, mods, mods, w_up, w_up, conv_w, conv_w, conv_b, conv_b, w_down, ln_w, ln_b)


def _pad_gate(w, row0):
    return jnp.pad(w, ((0, 0), (row0, LR_PAD - GATE_RANK - row0), (0, 0))).astype(BF16)


def kernel(x_prompt, x_sample, cache_k, cache_v, state_gla_fwd, state_gla_bwd, c, c_ctx, w_ada,
           b_ada, w_in, attn_sink, w_gate_f, b_gate_f, w_gate_b, b_gate_b, gla_norm_w, w_out,
           ln1_w, ln1_b, w_up, conv_w, conv_b, w_down, ln2_w, ln2_b):
    cc = jnp.concatenate([c_ctx[None, :], c, jnp.zeros((MOD_ROWS - 1 - DEC_BATCH, D_MODEL), F32)], axis=0)
    mods = _ada(cc, w_ada, b_ada).reshape(DEPTH * MOD_ROWS * N_MODS, 1, D_MODEL)
    cos_t, sin_t = _rope_tables()

    n_proj = PROJ_MAIN // TN_PROJ
    nf = D_FF // TF_FFN
    w_main = jnp.concatenate([w_in[:, :, a:b] for a, b in W_IN_ORDER], axis=2)
    w_main = w_main.reshape(DEPTH, D_MODEL, n_proj, TN_PROJ).transpose(0, 2, 1, 3).astype(BF16)
    w_lr = jnp.pad(w_in[:, :, PROJ_MAIN:], ((0, 0), (0, 0), (0, LR_PAD - 2 * GATE_RANK))).astype(BF16)
    w_up_b = w_up.reshape(DEPTH, D_MODEL, 2, nf, TF_FFN).transpose(0, 2, 3, 1, 4).astype(BF16)
    w_down_b = w_down.astype(BF16)
    w_out_b = w_out.astype(BF16)
    wgf = _pad_gate(w_gate_f, 0)
    wgb = _pad_gate(w_gate_b, GATE_RANK)
    bgf = b_gate_f[:, None, :]
    bgb = b_gate_b[:, None, :]
    norm_w = gla_norm_w[:, None, :]
    conv_b3 = conv_b[:, None, :]
    ln1_w3, ln1_b3, ln2_w3, ln2_b3 = (a[:, None, :] for a in (ln1_w, ln1_b, ln2_w, ln2_b))

    xs = {CTX: x_prompt.reshape(N_CTX, D_MODEL), LAT: x_sample.reshape(N_LAT, D_MODEL)}
    new_k, new_v = [], []
    states = (jnp.zeros((BATCH, DEPTH, GLA_HEADS, GLA_DK, GLA_DV), F32),) * 2
    for l in range(DEPTH):
        for path in (CTX, LAT):
            x = xs[path]
            proj, proj_lr = _in_proj(path, x, mods, w_main, w_lr, l)
            if path.is_ctx:
                o_att = _attn_ctx(proj, attn_sink, l)
                o_gla, sf, sb = _gla(path, proj, proj_lr, wgf, wgb, bgf, bgb, norm_w, l, states=states)
                states = (sf, sb)
                new_k.append(proj[:, COL_K:COL_K + KV_WIDTH].reshape(BATCH, SEQ, ATTN_KV_HEADS, HEAD_DIM))
                new_v.append(proj[:, COL_V:COL_V + KV_WIDTH].reshape(BATCH, SEQ, ATTN_KV_HEADS, HEAD_DIM))
            else:
                o_att = _attn_lat(proj, attn_sink, cache_k, cache_v, cos_t, sin_t, l)
                (o_gla,) = _gla(path, proj, proj_lr, wgf, wgb, bgf, bgb, norm_w, l,
                                s0=(state_gla_fwd, state_gla_bwd))
            x = _out_proj(path, o_att, o_gla, x, mods, w_out_b, ln1_w3, ln1_b3, l)
            xs[path] = _ffn(path, x, mods, w_up_b, conv_w, conv_b3, w_down_b, ln2_w3, ln2_b3, l)

    return (xs[CTX].reshape(BATCH, SEQ, D_MODEL), xs[LAT].reshape(DEC_BATCH, DEC_SEQ, D_MODEL),
            jnp.stack(new_k, axis=1), jnp.stack(new_v, axis=1), states[0], states[1])
```

```python
import functools

import jax
import jax.numpy as jnp
from jax import lax
from jax.experimental import pallas as pl
from jax.experimental.pallas import tpu as pltpu

F32 = jnp.float32
BF16 = jnp.bfloat16

D_MODEL = 2048
BATCH = 32
SEQ = 256
DEPTH = 2
DEC_BATCH = 4
DEC_SEQ = 2048
PAST_LEN = 256
GRID_W = 64
HEAD_DIM = 128
ATTN_WIDTH = 1024
ATTN_HEADS = 8
ATTN_KV_HEADS = 2
ATTN_GROUP = 4
KV_WIDTH = 256
WINDOW = 128
BLOCK = 128
GLA_WIDTH = 1024
GLA_HEADS = 4
GLA_DV = 256
GLA_DK = 128
GLA_KEY_WIDTH = 512
GATE_RANK = 16
GATE_NORM = 16.0
GLA_CHUNK = 64
GLA_GROUP = 256
GLA_CHAINS = 8
D_FF = 5632
ROPE_THETA = 10000.0
LN_EPS = 1e-5
NEG_INF = -1e30
DEEPNORM_ALPHA = (2 * DEPTH) ** 0.25

N_CTX = BATCH * SEQ
N_LAT = DEC_BATCH * DEC_SEQ
MOD_ROWS = 8
N_MODS = 6

PROJ_MAIN = 4608
COL_GV = ATTN_WIDTH
COL_GOG = COL_GV + GLA_WIDTH
COL_GQ = COL_GOG + GLA_WIDTH
COL_GK = COL_GQ + GLA_KEY_WIDTH
COL_K = COL_GK + GLA_KEY_WIDTH
COL_V = COL_K + KV_WIDTH
LR_PAD = 128
W_IN_ORDER = ((0, 1024), (2560, 3584), (3584, 4608), (1536, 2048), (2048, 2560), (1024, 1280),
              (1280, 1536))

VMEM_LIMIT = 56 * 1024 * 1024

TM_PROJ = 1024
TN_PROJ = 1536
TM_OUT = 512
TM_FFN = 1024
TF_FFN = 512
HALO = 16
TN_ADA = 1024


class Path:
    def __init__(self, name, n_batch, seq):
        self.name = name
        self.n_batch = n_batch
        self.seq = seq
        self.rows = n_batch * seq
        self.is_ctx = name == "ctx"

    def mod_row(self, i, tm):
        return 0 if self.is_ctx else 1 + i // (self.seq // tm)


CTX = Path("ctx", BATCH, SEQ)
LAT = Path("lat", DEC_BATCH, DEC_SEQ)


def _params(sem):
    return pltpu.CompilerParams(dimension_semantics=sem, vmem_limit_bytes=VMEM_LIMIT)


def _silu(x):
    return x / (1.0 + jnp.exp(-x))


def _layer_norm(y, w, b):
    mu = jnp.mean(y, axis=-1, keepdims=True)
    d = y - mu
    var = jnp.mean(d * d, axis=-1, keepdims=True)
    return d * lax.rsqrt(var + LN_EPS) * w + b


def _dot(a, b):
    return jnp.dot(a, b, preferred_element_type=F32)


def _dot_nt(a, b):
    return lax.dot_general(a, b, (((1,), (1,)), ((), ())), preferred_element_type=F32)


def _dot_tn(a, b):
    return lax.dot_general(a, b, (((0,), (0,)), ((), ())), preferred_element_type=F32)


def _mod_spec(path, layer, chunk, tm):
    def index_map(i, *_):
        return ((layer * MOD_ROWS + path.mod_row(i, tm)) * N_MODS + chunk, 0, 0)
    return pl.BlockSpec((1, 1, D_MODEL), index_map)


def _layer_vec_spec(layer, width, n_grid):
    if n_grid == 1:
        return pl.BlockSpec((1, 1, width), lambda i: (layer, 0, 0))
    return pl.BlockSpec((1, 1, width), lambda i, j: (layer, 0, 0))


def _ada_kernel(c_ref, w_ref, b_ref, o_ref):
    s = _silu(c_ref[...]).astype(BF16)
    o_ref[0] = _dot(s, w_ref[0].astype(BF16)) + b_ref[0]


def _ada(cc, w_ada, b_ada):
    n_out = N_MODS * D_MODEL
    return pl.pallas_call(
        _ada_kernel,
        out_shape=jax.ShapeDtypeStruct((DEPTH, MOD_ROWS, n_out), F32),
        grid=(DEPTH, n_out // TN_ADA),
        in_specs=[
            pl.BlockSpec((MOD_ROWS, D_MODEL), lambda l, j: (0, 0)),
            pl.BlockSpec((1, D_MODEL, TN_ADA), lambda l, j: (l, 0, j)),
            pl.BlockSpec((1, 1, TN_ADA), lambda l, j: (l, 0, j)),
        ],
        out_specs=pl.BlockSpec((1, MOD_ROWS, TN_ADA), lambda l, j: (l, 0, j)),
        compiler_params=_params(("arbitrary", "arbitrary")),
        name="ada",
    )(cc, w_ada, b_ada.reshape(DEPTH, 1, n_out))


def _inproj_kernel(x_ref, sh_ref, sc_ref, w_ref, wlr_ref, o_ref, olr_ref, h_scr):
    @pl.when(pl.program_id(1) == 0)
    def _():
        h = (x_ref[...] * (1.0 + sc_ref[0]) + sh_ref[0]).astype(BF16)
        h_scr[...] = h
        olr_ref[...] = _dot(h, wlr_ref[0])
    o_ref[...] = _dot(h_scr[...], w_ref[0])


def _in_proj(path, x, mods, w_main, w_lr, layer):
    tm, tn = TM_PROJ, TN_PROJ
    return pl.pallas_call(
        _inproj_kernel,
        out_shape=(jax.ShapeDtypeStruct((path.rows, PROJ_MAIN), F32),
                   jax.ShapeDtypeStruct((path.rows, LR_PAD), F32)),
        grid=(path.rows // tm, PROJ_MAIN // tn),
        in_specs=[
            pl.BlockSpec((tm, D_MODEL), lambda i, j: (i, 0)),
            _mod_spec(path, layer, 0, tm),
            _mod_spec(path, layer, 1, tm),
            pl.BlockSpec((1, D_MODEL, tn), lambda i, j: (layer, 0, j)),
            pl.BlockSpec((1, D_MODEL, LR_PAD), lambda i, j: (layer, 0, 0)),
        ],
        out_specs=(pl.BlockSpec((tm, tn), lambda i, j: (i, j)),
                   pl.BlockSpec((tm, LR_PAD), lambda i, j: (i, 0))),
        scratch_shapes=[pltpu.VMEM((tm, D_MODEL), BF16)],
        compiler_params=_params(("arbitrary", "arbitrary")),
        name="in_proj_" + path.name,
    )(x, mods, mods, w_main, w_lr)


def _sink_softmax_pv(sinks, scores, values):
    heads = range(len(sinks))
    ms = [functools.reduce(jnp.maximum, [jnp.max(s, axis=-1, keepdims=True) for s in scores[h]],
                           sinks[h]) for h in heads]
    es = [[jnp.exp(s - ms[h]) for s in scores[h]] for h in heads]
    dens = [functools.reduce(lambda a, e: a + jnp.sum(e, axis=-1, keepdims=True), es[h],
                             jnp.exp(sinks[h] - ms[h])) for h in heads]
    invs = [1.0 / d for d in dens]
    ps = [[(e * invs[h]).astype(BF16) for e in es[h]] for h in heads]
    return [functools.reduce(lambda a, b: a + b, [_dot(p, v) for p, v in zip(ps[h], values[h])])
            for h in heads]


def _attn_ctx_kernel(sink_ref, q_ref, k_ref, v_ref, o_ref, *, layer):
    scale = HEAD_DIM ** -0.5
    heads = range(ATTN_HEADS)
    ks = [k_ref[:, kh * HEAD_DIM:(kh + 1) * HEAD_DIM].astype(BF16) for kh in range(ATTN_KV_HEADS)]
    vs = [v_ref[:, kh * HEAD_DIM:(kh + 1) * HEAD_DIM].astype(BF16) for kh in range(ATTN_KV_HEADS)]
    scores = [[_dot_nt(q_ref[:, h * HEAD_DIM:(h + 1) * HEAD_DIM].astype(BF16),
                       ks[h // ATTN_GROUP]) * scale] for h in heads]
    outs = _sink_softmax_pv([sink_ref[layer, h] for h in heads], scores,
                            [[vs[h // ATTN_GROUP]] for h in heads])
    for h in heads:
        o_ref[:, h * HEAD_DIM:(h + 1) * HEAD_DIM] = outs[h].astype(BF16)


def _attn_ctx(proj, sink, layer):
    return pl.pallas_call(
        functools.partial(_attn_ctx_kernel, layer=layer),
        out_shape=jax.ShapeDtypeStruct((N_CTX, ATTN_WIDTH), BF16),
        grid=(BATCH,),
        in_specs=[
            pl.BlockSpec(memory_space=pltpu.SMEM),
            pl.BlockSpec((SEQ, ATTN_WIDTH), lambda b: (b, 0)),
            pl.BlockSpec((SEQ, KV_WIDTH), lambda b: (b, COL_K // KV_WIDTH)),
            pl.BlockSpec((SEQ, KV_WIDTH), lambda b: (b, COL_V // KV_WIDTH)),
        ],
        out_specs=pl.BlockSpec((SEQ, ATTN_WIDTH), lambda b: (b, 0)),
        compiler_params=_params(("arbitrary",)),
        name="attn_ctx",
    )(sink, proj, proj, proj)


def _attn_lat_kernel(sink_ref, q_ref, k_ref, v_ref, ck_ref, cv_ref, cos_ref, sin_ref, o_ref, *,
                     layer):
    scale = HEAD_DIM ** -0.5
    win = 3 * BLOCK
    i = pl.program_id(1)
    q0 = pl.multiple_of(i * BLOCK, BLOCK)
    start = pl.multiple_of(jnp.clip((i - 1) * BLOCK, 0, DEC_SEQ - win), BLOCK)
    cos_q, sin_q = cos_ref[pl.ds(q0, BLOCK), :], sin_ref[pl.ds(q0, BLOCK), :]
    cos_k, sin_k = cos_ref[pl.ds(start, win), :], sin_ref[pl.ds(start, win), :]
    lane = lax.broadcasted_iota(jnp.int32, (1, HEAD_DIM), 1)
    low = (lane % (HEAD_DIM // 2)) < (HEAD_DIM // 4)

    def rope(x, c, s):
        partner = jnp.where(low, pltpu.roll(x, HEAD_DIM - HEAD_DIM // 4, 1),
                            pltpu.roll(x, HEAD_DIM // 4, 1))
        return x * c + partner * s

    qpos = q0 + lax.broadcasted_iota(jnp.int32, (BLOCK, 1), 0)
    kpos = start + lax.broadcasted_iota(jnp.int32, (1, win), 1)
    valid = jnp.abs(qpos - kpos) <= WINDOW

    heads = range(ATTN_HEADS)
    kv_cols = [slice(kh * HEAD_DIM, (kh + 1) * HEAD_DIM) for kh in range(ATTN_KV_HEADS)]
    kws = [rope(k_ref[pl.ds(start, win), c], cos_k, sin_k).astype(BF16) for c in kv_cols]
    vws = [v_ref[pl.ds(start, win), c].astype(BF16) for c in kv_cols]
    cks = [ck_ref[0, 0, :, c].astype(BF16) for c in kv_cols]
    cvs = [cv_ref[0, 0, :, c].astype(BF16) for c in kv_cols]
    qs = [rope(q_ref[:, h * HEAD_DIM:(h + 1) * HEAD_DIM], cos_q, sin_q).astype(BF16) for h in heads]
    scores = [[jnp.where(valid, _dot_nt(qs[h], kws[h // ATTN_GROUP]) * scale, NEG_INF),
               _dot_nt(qs[h], cks[h // ATTN_GROUP]) * scale] for h in heads]
    outs = _sink_softmax_pv([sink_ref[layer, h] for h in heads], scores,
                            [[vws[h // ATTN_GROUP], cvs[h // ATTN_GROUP]] for h in heads])
    for h in heads:
        o_ref[:, h * HEAD_DIM:(h + 1) * HEAD_DIM] = outs[h].astype(BF16)


def _attn_lat(proj, sink, cache_k, cache_v, cos_t, sin_t, layer):
    nb = DEC_SEQ // BLOCK
    cache_spec = pl.BlockSpec((1, 1, PAST_LEN, KV_WIDTH), lambda b, i: (b, layer, 0, 0))
    table_spec = pl.BlockSpec((DEC_SEQ, HEAD_DIM), lambda b, i: (0, 0))
    return pl.pallas_call(
        functools.partial(_attn_lat_kernel, layer=layer),
        out_shape=jax.ShapeDtypeStruct((N_LAT, ATTN_WIDTH), BF16),
        grid=(DEC_BATCH, nb),
        in_specs=[
            pl.BlockSpec(memory_space=pltpu.SMEM),
            pl.BlockSpec((BLOCK, ATTN_WIDTH), lambda b, i: (b * nb + i, 0)),
            pl.BlockSpec((DEC_SEQ, KV_WIDTH), lambda b, i: (b, COL_K // KV_WIDTH)),
            pl.BlockSpec((DEC_SEQ, KV_WIDTH), lambda b, i: (b, COL_V // KV_WIDTH)),
            cache_spec, cache_spec, table_spec, table_spec,
        ],
        out_specs=pl.BlockSpec((BLOCK, ATTN_WIDTH), lambda b, i: (b * nb + i, 0)),
        compiler_params=_params(("arbitrary", "arbitrary")),
        name="attn_lat",
    )(sink, proj, proj, proj,
      cache_k.reshape(DEC_BATCH, DEPTH, PAST_LEN, KV_WIDTH),
      cache_v.reshape(DEC_BATCH, DEPTH, PAST_LEN, KV_WIDTH), cos_t, sin_t)


def _rope_tables():
    half = HEAD_DIM // 2
    n_freq = half // 2
    t = jnp.arange(DEC_SEQ)
    row = (t // GRID_W).astype(F32)
    col = (t % GRID_W).astype(F32)
    freqs = ROPE_THETA ** (-jnp.arange(n_freq, dtype=F32) / n_freq)
    tabs = []
    for pos in (row, col):
        ang = pos[:, None] * freqs[None, :]
        tabs.append((jnp.cos(ang), jnp.sin(ang)))
    cos_t = jnp.concatenate([tabs[0][0], tabs[0][0], tabs[1][0], tabs[1][0]], axis=-1)
    sin_t = jnp.concatenate([-tabs[0][1], tabs[0][1], -tabs[1][1], tabs[1][1]], axis=-1)
    return cos_t, sin_t


def _log_sigmoid(x):
    return jnp.minimum(x, 0.0) - jnp.log1p(jnp.exp(-jnp.abs(x)))


def _gla_kernel(*refs, seq, n_heads, is_ctx, gpi):
    q_ref, k_ref, v_ref, og_ref, lr_ref, wgf_ref, wgb_ref, bgf_ref, bgb_ref, nw_ref = refs[:10]
    if is_ctx:
        o_ref, sf_ref, sb_ref = refs[-16:-13]
    else:
        s0f_ref, s0b_ref = refs[10:12]
        o_ref = refs[-14]
    (gf_scr, gb_scr, qef_scr, qeb_scr, o_scr, uf_scr, ub_scr, sinf_scr, sinb_scr,
     decf_scr, decb_scr, stf_scr, stb_scr) = refs[-13:]
    c = GLA_CHUNK
    grp = GLA_GROUP
    cpg = grp // c
    n_chunks = seq // c
    n_groups = seq // grp
    q_scale = GLA_DK ** -0.5

    lr = lr_ref[...].astype(BF16)
    gf_scr[...] = _log_sigmoid(_dot(lr, wgf_ref[0]) + bgf_ref[0]) / GATE_NORM
    gb_scr[...] = _log_sigmoid(_dot(lr, wgb_ref[0]) + bgb_ref[0]) / GATE_NORM

    ri = lax.broadcasted_iota(jnp.int32, (grp, grp), 0)
    ci = lax.broadcasted_iota(jnp.int32, (grp, grp), 1)
    same_chunk = (ri // c) == (ci // c)
    keep_f = same_chunk & (ri >= ci)
    keep_b = same_chunk & (ri <= ci)

    def cumsum(keep, g):
        tri = keep.astype(BF16)
        g_hi = g.astype(BF16)
        r1 = g - g_hi.astype(F32)
        g_mid = r1.astype(BF16)
        g_lo = (r1 - g_mid.astype(F32)).astype(BF16)
        return _dot(jnp.concatenate([tri, tri, tri], axis=1),
                    jnp.concatenate([g_hi, g_mid, g_lo], axis=0))

    dirs = ((gf_scr, keep_f, c - 1, qef_scr, uf_scr, decf_scr),
            (gb_scr, keep_b, 0, qeb_scr, ub_scr, decb_scr))

    hks = [slice(h * GLA_DK, (h + 1) * GLA_DK) for h in range(n_heads)]
    hvs = [slice(h * GLA_DV, (h + 1) * GLA_DV) for h in range(n_heads)]

    def group_body(it, carry):
        items = [(it * gpi + gi, h) for gi in range(gpi) for h in range(n_heads)]
        rows = [pl.ds(pl.multiple_of(r * grp, grp), grp) for r, _ in items]
        chains = [(x, d) for x in range(len(items)) for d in range(2)]
        qs = [q_ref[rows[x], hks[h]] * q_scale for x, (_, h) in enumerate(items)]
        ks = [k_ref[rows[x], hks[h]] for x, (_, h) in enumerate(items)]
        vs = [v_ref[rows[x], hvs[h]].astype(BF16) for x, (_, h) in enumerate(items)]
        bs = [cumsum(dirs[d][1], dirs[d][0][rows[x], hks[items[x][1]]]) for x, d in chains]
        ebs = [jnp.exp(b) for b in bs]
        embs = [jnp.exp(-b) for b in bs]
        qes = [(qs[x] * ebs[i]).astype(BF16) for i, (x, d) in enumerate(chains)]
        kes = [ks[x] * embs[i] for i, (x, d) in enumerate(chains)]
        for i, (x, d) in enumerate(chains):
            dirs[d][3][rows[x], hks[items[x][1]]] = qes[i]
        avals = [jnp.where(dirs[d][1], _dot_nt(qes[i], kes[i].astype(BF16)), 0.0)
                 for i, (x, d) in enumerate(chains)]
        decays = [[jnp.exp(bs[i][cc * c + dirs[d][2]:cc * c + dirs[d][2] + 1, :])
                   for cc in range(cpg)] for i, (x, d) in enumerate(chains)]
        kds = [[(kes[i][cc * c:(cc + 1) * c] * decays[i][cc]).astype(BF16) for cc in range(cpg)]
               for i in range(len(chains))]
        for i, (x, d) in enumerate(chains):
            r, h = items[x]
            for cc in range(cpg):
                n = r * cpg + cc
                dirs[d][4][h, n] = _dot_tn(vs[x][cc * c:(cc + 1) * c], kds[i][cc])
                dirs[d][5][h, n] = jnp.broadcast_to(decays[i][cc], (8, GLA_DK))
        for x, (_, h) in enumerate(items):
            a_sum = (avals[2 * x] + avals[2 * x + 1]).astype(BF16)
            o_scr[rows[x], hvs[h]] = _dot(a_sum, vs[x])
        return carry

    lax.fori_loop(0, n_groups // gpi, group_body, 0)

    for h in range(n_heads):
        if is_ctx:
            stf_scr[h] = jnp.zeros((GLA_DV, GLA_DK), F32)
            stb_scr[h] = jnp.zeros((GLA_DV, GLA_DK), F32)
        else:
            stf_scr[h] = s0f_ref[0, 0, h].T
            stb_scr[h] = s0b_ref[0, 0, h].T

    def scan_body(n, carry):
        nb = n_chunks - 1 - n
        for h in range(n_heads):
            s = stf_scr[h]
            sinf_scr[h, n] = s.astype(BF16)
            stf_scr[h] = decf_scr[h, n][0:1, :] * s + uf_scr[h, n]
            s = stb_scr[h]
            sinb_scr[h, nb] = s.astype(BF16)
            stb_scr[h] = decb_scr[h, nb][0:1, :] * s + ub_scr[h, nb]
        return carry

    lax.fori_loop(0, n_chunks, scan_body, 0)
    if is_ctx:
        for h in range(n_heads):
            sf_ref[0, 0, h] = stf_scr[h].T
            sb_ref[0, 0, h] = stb_scr[h].T

    nw = nw_ref[0]

    def finish(it, carry):
        items = [(it * gpi + gi, h) for gi in range(gpi) for h in range(n_heads)]
        rows = [pl.ds(pl.multiple_of(r * grp, grp), grp) for r, _ in items]
        inter = []
        for r, h in items:
            parts = []
            for cc in range(cpg):
                n = r * cpg + cc
                cr = pl.ds(pl.multiple_of(r * grp + cc * c, c), c)
                qe2 = jnp.concatenate([qef_scr[cr, hks[h]], qeb_scr[cr, hks[h]]], axis=1)
                s2 = jnp.concatenate([sinf_scr[h, n], sinb_scr[h, n]], axis=1)
                parts.append(_dot_nt(qe2, s2))
            inter.append(jnp.concatenate(parts, axis=0))
        os = [o_scr[rows[x], hvs[h]] + inter[x] for x, (_, h) in enumerate(items)]
        scales = [lax.rsqrt(jnp.mean(o * o, axis=-1, keepdims=True) + LN_EPS) for o in os]
        gates = [_silu(og_ref[rows[x], hvs[h]]) for x, (_, h) in enumerate(items)]
        for x, (_, h) in enumerate(items):
            o_ref[rows[x], hvs[h]] = (os[x] * scales[x] * nw * gates[x]).astype(BF16)
        return carry

    lax.fori_loop(0, n_groups // gpi, finish, 0)


def _gla(path, proj, proj_lr, wgf, wgb, bgf, bgb, norm_w, layer, *, s0=None, states=None):
    seq = path.seq
    n_heads = GLA_HEADS if path.is_ctx else 1
    n_chunks = seq // GLA_CHUNK
    n_groups = seq // GLA_GROUP

    def col_spec(width, col):
        w = n_heads * width
        return pl.BlockSpec((seq, w), lambda b, h: (b, col // w + h))

    head_w = pl.BlockSpec((1, LR_PAD, n_heads * GLA_DK), lambda b, h: (layer, 0, h))
    head_b = pl.BlockSpec((1, 1, n_heads * GLA_DK), lambda b, h: (layer, 0, h))
    state_spec = pl.BlockSpec((1, 1, n_heads, GLA_DK, GLA_DV), lambda b, h: (b, layer, h, 0, 0))
    in_specs = [
        col_spec(GLA_DK, COL_GQ), col_spec(GLA_DK, COL_GK), col_spec(GLA_DV, COL_GV),
        col_spec(GLA_DV, COL_GOG),
        pl.BlockSpec((seq, LR_PAD), lambda b, h: (b, 0)),
        head_w, head_w, head_b, head_b,
        pl.BlockSpec((1, 1, GLA_DV), lambda b, h: (layer, 0, 0)),
    ]
    args = [proj, proj, proj, proj, proj_lr, wgf, wgb, bgf, bgb, norm_w]
    o_shape = jax.ShapeDtypeStruct((path.rows, GLA_WIDTH), BF16)
    o_spec = pl.BlockSpec((seq, n_heads * GLA_DV), lambda b, h: (b, h))
    aliases = {}
    if path.is_ctx:
        state_shape = jax.ShapeDtypeStruct((BATCH, DEPTH, GLA_HEADS, GLA_DK, GLA_DV), F32)
        out_shape = (o_shape, state_shape, state_shape)
        out_specs = (o_spec, state_spec, state_spec)
        in_specs += [pl.BlockSpec(memory_space=pl.ANY), pl.BlockSpec(memory_space=pl.ANY)]
        args += list(states)
        aliases = {10: 1, 11: 2}
    else:
        in_specs += [state_spec, state_spec]
        args += list(s0)
        out_shape = (o_shape,)
        out_specs = (o_spec,)
    per_chunk = (n_heads, n_chunks, GLA_DV, GLA_DK)
    return pl.pallas_call(
        functools.partial(_gla_kernel, seq=seq, n_heads=n_heads, is_ctx=path.is_ctx,
                          gpi=min(n_groups, GLA_CHAINS // (2 * n_heads))),
        out_shape=out_shape,
        grid=(path.n_batch, GLA_HEADS // n_heads),
        in_specs=in_specs,
        out_specs=out_specs,
        input_output_aliases=aliases,
        scratch_shapes=[
            pltpu.VMEM((seq, n_heads * GLA_DK), F32), pltpu.VMEM((seq, n_heads * GLA_DK), F32),
            pltpu.VMEM((seq, n_heads * GLA_DK), BF16), pltpu.VMEM((seq, n_heads * GLA_DK), BF16),
            pltpu.VMEM((seq, n_heads * GLA_DV), F32),
            pltpu.VMEM(per_chunk, F32), pltpu.VMEM(per_chunk, F32),
            pltpu.VMEM(per_chunk, BF16), pltpu.VMEM(per_chunk, BF16),
            pltpu.VMEM((n_heads, n_chunks, 8, GLA_DK), F32),
            pltpu.VMEM((n_heads, n_chunks, 8, GLA_DK), F32),
            pltpu.VMEM((n_heads, GLA_DV, GLA_DK), F32), pltpu.VMEM((n_heads, GLA_DV, GLA_DK), F32),
        ],
        compiler_params=_params(("arbitrary", "arbitrary")),
        name="gla_" + path.name,
    )(*args)


def _outproj_kernel(oa_ref, og_ref, x_ref, g_ref, w_ref, lw_ref, lb_ref, o_ref):
    f = _dot(oa_ref[...], w_ref[0, :ATTN_WIDTH, :]) + _dot(og_ref[...], w_ref[0, ATTN_WIDTH:, :])
    y = DEEPNORM_ALPHA * x_ref[...] + g_ref[0] * f
    o_ref[...] = _layer_norm(y, lw_ref[0], lb_ref[0])


def _out_proj(path, o_att, o_gla, x, mods, w_out, ln_w, ln_b, layer):
    tm = TM_OUT
    vec = _layer_vec_spec(layer, D_MODEL, 1)
    return pl.pallas_call(
        _outproj_kernel,
        out_shape=jax.ShapeDtypeStruct((path.rows, D_MODEL), F32),
        grid=(path.rows // tm,),
        in_specs=[
            pl.BlockSpec((tm, ATTN_WIDTH), lambda i: (i, 0)),
            pl.BlockSpec((tm, GLA_WIDTH), lambda i: (i, 0)),
            pl.BlockSpec((tm, D_MODEL), lambda i: (i, 0)),
            _mod_spec(path, layer, 2, tm),
            pl.BlockSpec((1, D_MODEL, D_MODEL), lambda i: (layer, 0, 0)),
            vec, vec,
        ],
        out_specs=pl.BlockSpec((tm, D_MODEL), lambda i: (i, 0)),
        compiler_params=_params(("arbitrary",)),
        name="out_proj_" + path.name,
    )(o_att, o_gla, x, mods, w_out, ln_w, ln_b)


def _ffn_kernel(*refs, tm, nf, seq, has_halo):
    if has_halo:
        x_ref, xp_ref, xn_ref = refs[:3]
        refs = refs[3:]
    else:
        x_ref = refs[0]
        refs = refs[1:]
    (sh_ref, sc_ref, g_ref, wa_ref, wg_ref, cwa_ref, cwg_ref, cba_ref, cbg_ref, wd_ref, lw_ref,
     lb_ref, o_ref, h_scr, ua0_scr, ug0_scr, ua1_scr, ug1_scr, act0_scr, act1_scr) = refs
    i = pl.program_id(0)
    j = pl.program_id(1)
    hrows = HALO if has_halo else 0
    u0 = HALO - hrows
    u_slots = ((ua0_scr, ug0_scr), (ua1_scr, ug1_scr))
    act_slots = (act0_scr, act1_scr)

    tf = wa_ref.shape[-1]
    h_rows = tm + 2 * hrows
    up_n, up_m = tf // 2, h_rows // 2
    act_r, act_c = 128, 128
    down_m, down_n = tm // 2, D_MODEL // 4

    def up_piece(slot, which, n0, m0):
        w_ref = (wa_ref, wg_ref)[which]
        u_slots[slot][which][u0 + m0:u0 + m0 + up_m, n0:n0 + up_n] = _dot(
            h_scr[m0:m0 + up_m, :], w_ref[0, :, n0:n0 + up_n])

    def down_piece(slot, m0, n0):
        o_ref[m0:m0 + down_m, n0:n0 + down_n] += _dot(act_slots[slot][m0:m0 + down_m, :],
                                                      wd_ref[0, :, n0:n0 + down_n])

    def act_piece(slot, r0, c0):
        cols = slice(c0, c0 + act_c)
        pos = (i * tm + r0 + lax.broadcasted_iota(jnp.int32, (act_r, 1), 0)) & (seq - 1)
        first = pos == 0
        last = pos == seq - 1

        def conv(u_ref, cw_ref, cb_ref):
            cw = cw_ref[0, :, cols]
            prev = u_ref[HALO - 1 + r0:HALO - 1 + r0 + act_r, cols]
            cur = u_ref[HALO + r0:HALO + r0 + act_r, cols]
            nxt = u_ref[HALO + 1 + r0:HALO + 1 + r0 + act_r, cols]
            return (jnp.where(first, 0.0, prev) * cw[0:1] + cur * cw[1:2]
                    + jnp.where(last, 0.0, nxt) * cw[2:3] + cb_ref[0, :, cols])

        a = conv(u_slots[slot][0], cwa_ref, cba_ref)
        g = conv(u_slots[slot][1], cwg_ref, cbg_ref)
        act_slots[slot][r0:r0 + act_r, cols] = (_silu(g) * a).astype(BF16)

    def run(up=None, act=None, down=None):
        pieces = []

        def add(fn, arg_list):
            for k, a in enumerate(arg_list):
                pieces.append(((k + 0.5) / len(arg_list), len(pieces), fn, a))

        if up is not None:
            add(up_piece, [(up, w, n0, m0) for w in (0, 1) for n0 in range(0, tf, up_n)
                           for m0 in range(0, h_rows, up_m)])
        if act is not None:
            add(act_piece, [(act, r0, c0) for c0 in range(0, tf, act_c)
                            for r0 in range(0, tm, act_r)])
        if down is not None:
            add(down_piece, [(down, m0, n0) for n0 in range(0, D_MODEL, down_n)
                             for m0 in range(0, tm, down_m)])
        for _, _, fn, a in sorted(pieces, key=lambda t: t[:2]):
            fn(*a)

    @pl.when(j == 0)
    def _():
        sc = 1.0 + sc_ref[0]
        sh = sh_ref[0]
        h_scr[hrows:hrows + tm, :] = (x_ref[...] * sc + sh).astype(BF16)
        if has_halo:
            h_scr[0:HALO, :] = (xp_ref[...] * sc + sh).astype(BF16)
            h_scr[HALO + tm:2 * HALO + tm, :] = (xn_ref[...] * sc + sh).astype(BF16)
        else:
            for pair in u_slots:
                for u_scr in pair:
                    u_scr[0:HALO, :] = jnp.zeros((HALO, tf), F32)
                    u_scr[HALO + tm:2 * HALO + tm, :] = jnp.zeros((HALO, tf), F32)
        o_ref[...] = jnp.zeros_like(o_ref)
        run(up=0)

    @pl.when(j == 1)
    def _():
        run(up=1, act=0)

    for parity in (0, 1):
        @pl.when((j >= 2) & (j < nf) & (j % 2 == parity))
        def _(parity=parity):
            run(up=parity, act=1 - parity, down=parity)

    @pl.when(j == nf)
    def _():
        run(act=(nf - 1) % 2, down=nf % 2)

    @pl.when(j == nf + 1)
    def _():
        run(down=(nf - 1) % 2)
        y = DEEPNORM_ALPHA * x_ref[...] + g_ref[0] * o_ref[...]
        o_ref[...] = _layer_norm(y, lw_ref[0], lb_ref[0])


def _ffn(path, x, mods, w_up, conv_w, conv_b, w_down, ln_w, ln_b, layer):
    tm, tf = TM_FFN, TF_FFN
    nf = D_FF // tf
    hb = tm // HALO
    n_halo_blocks = path.rows // HALO
    has_halo = path.seq > tm
    vec = _layer_vec_spec(layer, D_MODEL, 2)

    def chunk(j, lag):
        return jnp.clip(j - lag, 0, nf - 1)

    x_specs = [pl.BlockSpec((tm, D_MODEL), lambda i, j: (i, 0))]
    x_args = [x]
    if has_halo:
        x_specs += [
            pl.BlockSpec((HALO, D_MODEL), lambda i, j: (jnp.maximum(i * hb - 1, 0), 0)),
            pl.BlockSpec((HALO, D_MODEL),
                         lambda i, j: (jnp.minimum((i + 1) * hb, n_halo_blocks - 1), 0)),
        ]
        x_args += [x, x]
    h_rows = tm + 2 * HALO if has_halo else tm
    u_scratch = pltpu.VMEM((tm + 2 * HALO, tf), F32)
    act_scratch = pltpu.VMEM((tm, tf), BF16)
    return pl.pallas_call(
        functools.partial(_ffn_kernel, tm=tm, nf=nf, seq=path.seq, has_halo=has_halo),
        out_shape=jax.ShapeDtypeStruct((path.rows, D_MODEL), F32),
        grid=(path.rows // tm, nf + 2),
        in_specs=x_specs + [
            _mod_spec(path, layer, 3, tm), _mod_spec(path, layer, 4, tm),
            _mod_spec(path, layer, 5, tm),
            pl.BlockSpec((1, D_MODEL, tf), lambda i, j: (layer, 0, chunk(j, 0))),
            pl.BlockSpec((1, D_MODEL, tf), lambda i, j: (layer, 0, nf + chunk(j, 0))),
            pl.BlockSpec((1, 3, tf), lambda i, j: (layer, 0, chunk(j, 1))),
            pl.BlockSpec((1, 3, tf), lambda i, j: (layer, 0, nf + chunk(j, 1))),
            pl.BlockSpec((1, 1, tf), lambda i, j: (layer, 0, chunk(j, 1))),
            pl.BlockSpec((1, 1, tf), lambda i, j: (layer, 0, nf + chunk(j, 1))),
            pl.BlockSpec((1, tf, D_MODEL), lambda i, j: (layer, chunk(j, 2), 0)),
            vec, vec,
        ],
        out_specs=pl.BlockSpec((tm, D_MODEL), lambda i, j: (i, 0), pipeline_mode=pl.Buffered(1)),
        scratch_shapes=[pltpu.VMEM((h_rows, D_MODEL), BF16),
                        u_scratch, u_scratch, u_scratch, u_scratch, act_scratch, act_scratch],
        compiler_params=_params(("arbitrary", "arbitrary")),
        name="ffn_" + path.name,
    )(*x_args, mods, mods, mods, w_up, w_up, conv_w, conv_w, conv_b, conv_b, w_down, ln_w, ln_b)


def _pad_gate(w, row0):
    return jnp.pad(w, ((0, 0), (row0, LR_PAD - GATE_RANK - row0), (0, 0))).astype(BF16)


def kernel(x_prompt, x_sample, cache_k, cache_v, state_gla_fwd, state_gla_bwd, c, c_ctx, w_ada,
           b_ada, w_in, attn_sink, w_gate_f, b_gate_f, w_gate_b, b_gate_b, gla_norm_w, w_out,
           ln1_w, ln1_b, w_up, conv_w, conv_b, w_down, ln2_w, ln2_b):
    cc = jnp.concatenate([c_ctx[None, :], c, jnp.zeros((MOD_ROWS - 1 - DEC_BATCH, D_MODEL), F32)], axis=0)
    mods = _ada(cc, w_ada, b_ada).reshape(DEPTH * MOD_ROWS * N_MODS, 1, D_MODEL)
    cos_t, sin_t = _rope_tables()

    w_main = jnp.concatenate([w_in[:, :, a:b] for a, b in W_IN_ORDER], axis=2).astype(BF16)
    w_lr = jnp.pad(w_in[:, :, PROJ_MAIN:], ((0, 0), (0, 0), (0, LR_PAD - 2 * GATE_RANK))).astype(BF16)
    w_up_b = w_up.astype(BF16)
    w_down_b = w_down.astype(BF16)
    w_out_b = w_out.astype(BF16)
    wgf = _pad_gate(w_gate_f, 0)
    wgb = _pad_gate(w_gate_b, GATE_RANK)
    bgf = b_gate_f[:, None, :]
    bgb = b_gate_b[:, None, :]
    norm_w = gla_norm_w[:, None, :]
    conv_b3 = conv_b[:, None, :]
    ln1_w3, ln1_b3, ln2_w3, ln2_b3 = (a[:, None, :] for a in (ln1_w, ln1_b, ln2_w, ln2_b))

    xs = {CTX: x_prompt.reshape(N_CTX, D_MODEL), LAT: x_sample.reshape(N_LAT, D_MODEL)}
    new_k, new_v = [], []
    states = (jnp.zeros((BATCH, DEPTH, GLA_HEADS, GLA_DK, GLA_DV), F32),) * 2
    for l in range(DEPTH):
        for path in (CTX, LAT):
            x = xs[path]
            proj, proj_lr = _in_proj(path, x, mods, w_main, w_lr, l)
            if path.is_ctx:
                o_att = _attn_ctx(proj, attn_sink, l)
                o_gla, sf, sb = _gla(path, proj, proj_lr, wgf, wgb, bgf, bgb, norm_w, l, states=states)
                states = (sf, sb)
                new_k.append(proj[:, COL_K:COL_K + KV_WIDTH].reshape(BATCH, SEQ, ATTN_KV_HEADS, HEAD_DIM))
                new_v.append(proj[:, COL_V:COL_V + KV_WIDTH].reshape(BATCH, SEQ, ATTN_KV_HEADS, HEAD_DIM))
            else:
                o_att = _attn_lat(proj, attn_sink, cache_k, cache_v, cos_t, sin_t, l)
                (o_gla,) = _gla(path, proj, proj_lr, wgf, wgb, bgf, bgb, norm_w, l,
                                s0=(state_gla_fwd, state_gla_bwd))
            x = _out_proj(path, o_att, o_gla, x, mods, w_out_b, ln1_w3, ln1_b3, l)
            xs[path] = _ffn(path, x, mods, w_up_b, conv_w, conv_b3, w_down_b, ln2_w3, ln2_b3, l)

    return (xs[CTX].reshape(BATCH, SEQ, D_MODEL), xs[LAT].reshape(DEC_BATCH, DEC_SEQ, D_MODEL),
            jnp.stack(new_k, axis=1), jnp.stack(new_v, axis=1), states[0], states[1])
```

```python
import functools

import jax
import jax.numpy as jnp
from jax import lax
from jax.experimental import pallas as pl
from jax.experimental.pallas import tpu as pltpu

F32 = jnp.float32
BF16 = jnp.bfloat16

D_MODEL = 2048
BATCH = 32
SEQ = 256
DEPTH = 2
DEC_BATCH = 4
DEC_SEQ = 2048
PAST_LEN = 256
GRID_W = 64
HEAD_DIM = 128
ATTN_WIDTH = 1024
ATTN_HEADS = 8
ATTN_KV_HEADS = 2
ATTN_GROUP = 4
KV_WIDTH = 256
WINDOW = 128
BLOCK = 128
GLA_WIDTH = 1024
GLA_HEADS = 4
GLA_DV = 256
GLA_DK = 128
GLA_KEY_WIDTH = 512
GATE_RANK = 16
GATE_NORM = 16.0
GLA_CHUNK = 64
GLA_GROUP = 256
GLA_CHAINS = 8
D_FF = 5632
ROPE_THETA = 10000.0
LN_EPS = 1e-5
NEG_INF = -1e30
DEEPNORM_ALPHA = (2 * DEPTH) ** 0.25

N_CTX = BATCH * SEQ
N_LAT = DEC_BATCH * DEC_SEQ
MOD_ROWS = 8
N_MODS = 6

PROJ_MAIN = 4608
COL_GV = ATTN_WIDTH
COL_GOG = COL_GV + GLA_WIDTH
COL_GQ = COL_GOG + GLA_WIDTH
COL_GK = COL_GQ + GLA_KEY_WIDTH
COL_K = COL_GK + GLA_KEY_WIDTH
COL_V = COL_K + KV_WIDTH
LR_PAD = 128
W_IN_ORDER = ((0, 1024), (2560, 3584), (3584, 4608), (1536, 2048), (2048, 2560), (1024, 1280),
              (1280, 1536))

VMEM_LIMIT = 56 * 1024 * 1024

TM_PROJ = 1024
TN_PROJ = 1536
TM_OUT = 512
TM_FFN = 1024
TF_FFN = 512
HALO = 16
GAP = 8
TN_ADA = 1024


class Path:
    def __init__(self, name, n_batch, seq):
        self.name = name
        self.n_batch = n_batch
        self.seq = seq
        self.rows = n_batch * seq
        self.is_ctx = name == "ctx"

    def mod_row(self, i, tm):
        return 0 if self.is_ctx else 1 + i // (self.seq // tm)


CTX = Path("ctx", BATCH, SEQ)
LAT = Path("lat", DEC_BATCH, DEC_SEQ)


def _params(sem):
    return pltpu.CompilerParams(dimension_semantics=sem, vmem_limit_bytes=VMEM_LIMIT)


def _silu(x):
    return x / (1.0 + jnp.exp(-x))


def _layer_norm(y, w, b):
    mu = jnp.mean(y, axis=-1, keepdims=True)
    d = y - mu
    var = jnp.mean(d * d, axis=-1, keepdims=True)
    return d * lax.rsqrt(var + LN_EPS) * w + b


def _dot(a, b):
    return jnp.dot(a, b, preferred_element_type=F32)


def _dot_nt(a, b):
    return lax.dot_general(a, b, (((1,), (1,)), ((), ())), preferred_element_type=F32)


def _dot_tn(a, b):
    return lax.dot_general(a, b, (((0,), (0,)), ((), ())), preferred_element_type=F32)


def _mod_spec(path, layer, chunk, tm):
    def index_map(i, *_):
        return ((layer * MOD_ROWS + path.mod_row(i, tm)) * N_MODS + chunk, 0, 0)
    return pl.BlockSpec((1, 1, D_MODEL), index_map)


def _layer_vec_spec(layer, width, n_grid):
    if n_grid == 1:
        return pl.BlockSpec((1, 1, width), lambda i: (layer, 0, 0))
    return pl.BlockSpec((1, 1, width), lambda i, j: (layer, 0, 0))


def _ada_kernel(c_ref, w_ref, b_ref, o_ref):
    s = _silu(c_ref[...]).astype(BF16)
    o_ref[0] = _dot(s, w_ref[0].astype(BF16)) + b_ref[0]


def _ada(cc, w_ada, b_ada):
    n_out = N_MODS * D_MODEL
    return pl.pallas_call(
        _ada_kernel,
        out_shape=jax.ShapeDtypeStruct((DEPTH, MOD_ROWS, n_out), F32),
        grid=(DEPTH, n_out // TN_ADA),
        in_specs=[
            pl.BlockSpec((MOD_ROWS, D_MODEL), lambda l, j: (0, 0)),
            pl.BlockSpec((1, D_MODEL, TN_ADA), lambda l, j: (l, 0, j)),
            pl.BlockSpec((1, 1, TN_ADA), lambda l, j: (l, 0, j)),
        ],
        out_specs=pl.BlockSpec((1, MOD_ROWS, TN_ADA), lambda l, j: (l, 0, j)),
        compiler_params=_params(("arbitrary", "arbitrary")),
        name="ada",
    )(cc, w_ada, b_ada.reshape(DEPTH, 1, n_out))


def _inproj_kernel(x_ref, sh_ref, sc_ref, w_ref, wlr_ref, o_ref, olr_ref, h_scr):
    @pl.when(pl.program_id(1) == 0)
    def _():
        h = (x_ref[...] * (1.0 + sc_ref[0]) + sh_ref[0]).astype(BF16)
        h_scr[...] = h
        olr_ref[...] = _dot(h, wlr_ref[0])
    o_ref[...] = _dot(h_scr[...], w_ref[0])


def _in_proj(path, x, mods, w_main, w_lr, layer):
    tm, tn = TM_PROJ, TN_PROJ
    return pl.pallas_call(
        _inproj_kernel,
        out_shape=(jax.ShapeDtypeStruct((path.rows, PROJ_MAIN), F32),
                   jax.ShapeDtypeStruct((path.rows, LR_PAD), F32)),
        grid=(path.rows // tm, PROJ_MAIN // tn),
        in_specs=[
            pl.BlockSpec((tm, D_MODEL), lambda i, j: (i, 0)),
            _mod_spec(path, layer, 0, tm),
            _mod_spec(path, layer, 1, tm),
            pl.BlockSpec((1, D_MODEL, tn), lambda i, j: (layer, 0, j)),
            pl.BlockSpec((1, D_MODEL, LR_PAD), lambda i, j: (layer, 0, 0)),
        ],
        out_specs=(pl.BlockSpec((tm, tn), lambda i, j: (i, j)),
                   pl.BlockSpec((tm, LR_PAD), lambda i, j: (i, 0))),
        scratch_shapes=[pltpu.VMEM((tm, D_MODEL), BF16)],
        compiler_params=_params(("arbitrary", "arbitrary")),
        name="in_proj_" + path.name,
    )(x, mods, mods, w_main, w_lr)


def _sink_softmax_pv(sinks, scores, values):
    heads = range(len(sinks))
    ms = [functools.reduce(jnp.maximum, [jnp.max(s, axis=-1, keepdims=True) for s in scores[h]],
                           sinks[h]) for h in heads]
    es = [[jnp.exp(s - ms[h]) for s in scores[h]] for h in heads]
    dens = [functools.reduce(lambda a, e: a + jnp.sum(e, axis=-1, keepdims=True), es[h],
                             jnp.exp(sinks[h] - ms[h])) for h in heads]
    invs = [1.0 / d for d in dens]
    ps = [[(e * invs[h]).astype(BF16) for e in es[h]] for h in heads]
    return [functools.reduce(lambda a, b: a + b, [_dot(p, v) for p, v in zip(ps[h], values[h])])
            for h in heads]


def _attn_ctx_kernel(sink_ref, q_ref, k_ref, v_ref, o_ref, *, layer):
    scale = HEAD_DIM ** -0.5
    heads = range(ATTN_HEADS)
    ks = [k_ref[:, kh * HEAD_DIM:(kh + 1) * HEAD_DIM].astype(BF16) for kh in range(ATTN_KV_HEADS)]
    vs = [v_ref[:, kh * HEAD_DIM:(kh + 1) * HEAD_DIM].astype(BF16) for kh in range(ATTN_KV_HEADS)]
    scores = [[_dot_nt(q_ref[:, h * HEAD_DIM:(h + 1) * HEAD_DIM].astype(BF16),
                       ks[h // ATTN_GROUP]) * scale] for h in heads]
    outs = _sink_softmax_pv([sink_ref[layer, h] for h in heads], scores,
                            [[vs[h // ATTN_GROUP]] for h in heads])
    for h in heads:
        o_ref[:, h * HEAD_DIM:(h + 1) * HEAD_DIM] = outs[h].astype(BF16)


def _attn_ctx(proj, sink, layer):
    return pl.pallas_call(
        functools.partial(_attn_ctx_kernel, layer=layer),
        out_shape=jax.ShapeDtypeStruct((N_CTX, ATTN_WIDTH), BF16),
        grid=(BATCH,),
        in_specs=[
            pl.BlockSpec(memory_space=pltpu.SMEM),
            pl.BlockSpec((SEQ, ATTN_WIDTH), lambda b: (b, 0)),
            pl.BlockSpec((SEQ, KV_WIDTH), lambda b: (b, COL_K // KV_WIDTH)),
            pl.BlockSpec((SEQ, KV_WIDTH), lambda b: (b, COL_V // KV_WIDTH)),
        ],
        out_specs=pl.BlockSpec((SEQ, ATTN_WIDTH), lambda b: (b, 0)),
        compiler_params=_params(("arbitrary",)),
        name="attn_ctx",
    )(sink, proj, proj, proj)


def _attn_lat_kernel(sink_ref, q_ref, k_ref, v_ref, ck_ref, cv_ref, cos_ref, sin_ref, o_ref, *,
                     layer):
    scale = HEAD_DIM ** -0.5
    win = 3 * BLOCK
    i = pl.program_id(1)
    q0 = pl.multiple_of(i * BLOCK, BLOCK)
    start = pl.multiple_of(jnp.clip((i - 1) * BLOCK, 0, DEC_SEQ - win), BLOCK)
    cos_q, sin_q = cos_ref[pl.ds(q0, BLOCK), :], sin_ref[pl.ds(q0, BLOCK), :]
    cos_k, sin_k = cos_ref[pl.ds(start, win), :], sin_ref[pl.ds(start, win), :]
    lane = lax.broadcasted_iota(jnp.int32, (1, HEAD_DIM), 1)
    low = (lane % (HEAD_DIM // 2)) < (HEAD_DIM // 4)

    def rope(x, c, s):
        partner = jnp.where(low, pltpu.roll(x, HEAD_DIM - HEAD_DIM // 4, 1),
                            pltpu.roll(x, HEAD_DIM // 4, 1))
        return x * c + partner * s

    qpos = q0 + lax.broadcasted_iota(jnp.int32, (BLOCK, 1), 0)
    kpos = start + lax.broadcasted_iota(jnp.int32, (1, win), 1)
    valid = jnp.abs(qpos - kpos) <= WINDOW

    heads = range(ATTN_HEADS)
    kv_cols = [slice(kh * HEAD_DIM, (kh + 1) * HEAD_DIM) for kh in range(ATTN_KV_HEADS)]
    kws = [rope(k_ref[pl.ds(start, win), c], cos_k, sin_k).astype(BF16) for c in kv_cols]
    vws = [v_ref[pl.ds(start, win), c].astype(BF16) for c in kv_cols]
    cks = [ck_ref[0, 0, :, c].astype(BF16) for c in kv_cols]
    cvs = [cv_ref[0, 0, :, c].astype(BF16) for c in kv_cols]
    qs = [rope(q_ref[:, h * HEAD_DIM:(h + 1) * HEAD_DIM], cos_q, sin_q).astype(BF16) for h in heads]
    scores = [[jnp.where(valid, _dot_nt(qs[h], kws[h // ATTN_GROUP]) * scale, NEG_INF),
               _dot_nt(qs[h], cks[h // ATTN_GROUP]) * scale] for h in heads]
    outs = _sink_softmax_pv([sink_ref[layer, h] for h in heads], scores,
                            [[vws[h // ATTN_GROUP], cvs[h // ATTN_GROUP]] for h in heads])
    for h in heads:
        o_ref[:, h * HEAD_DIM:(h + 1) * HEAD_DIM] = outs[h].astype(BF16)


def _attn_lat(proj, sink, cache_k, cache_v, cos_t, sin_t, layer):
    nb = DEC_SEQ // BLOCK
    cache_spec = pl.BlockSpec((1, 1, PAST_LEN, KV_WIDTH), lambda b, i: (b, layer, 0, 0))
    table_spec = pl.BlockSpec((DEC_SEQ, HEAD_DIM), lambda b, i: (0, 0))
    return pl.pallas_call(
        functools.partial(_attn_lat_kernel, layer=layer),
        out_shape=jax.ShapeDtypeStruct((N_LAT, ATTN_WIDTH), BF16),
        grid=(DEC_BATCH, nb),
        in_specs=[
            pl.BlockSpec(memory_space=pltpu.SMEM),
            pl.BlockSpec((BLOCK, ATTN_WIDTH), lambda b, i: (b * nb + i, 0)),
            pl.BlockSpec((DEC_SEQ, KV_WIDTH), lambda b, i: (b, COL_K // KV_WIDTH)),
            pl.BlockSpec((DEC_SEQ, KV_WIDTH), lambda b, i: (b, COL_V // KV_WIDTH)),
            cache_spec, cache_spec, table_spec, table_spec,
        ],
        out_specs=pl.BlockSpec((BLOCK, ATTN_WIDTH), lambda b, i: (b * nb + i, 0)),
        compiler_params=_params(("arbitrary", "arbitrary")),
        name="attn_lat",
    )(sink, proj, proj, proj,
      cache_k.reshape(DEC_BATCH, DEPTH, PAST_LEN, KV_WIDTH),
      cache_v.reshape(DEC_BATCH, DEPTH, PAST_LEN, KV_WIDTH), cos_t, sin_t)


def _rope_tables():
    half = HEAD_DIM // 2
    n_freq = half // 2
    t = jnp.arange(DEC_SEQ)
    row = (t // GRID_W).astype(F32)
    col = (t % GRID_W).astype(F32)
    freqs = ROPE_THETA ** (-jnp.arange(n_freq, dtype=F32) / n_freq)
    tabs = []
    for pos in (row, col):
        ang = pos[:, None] * freqs[None, :]
        tabs.append((jnp.cos(ang), jnp.sin(ang)))
    cos_t = jnp.concatenate([tabs[0][0], tabs[0][0], tabs[1][0], tabs[1][0]], axis=-1)
    sin_t = jnp.concatenate([-tabs[0][1], tabs[0][1], -tabs[1][1], tabs[1][1]], axis=-1)
    return cos_t, sin_t


def _log_sigmoid(x):
    return jnp.minimum(x, 0.0) - jnp.log1p(jnp.exp(-jnp.abs(x)))


def _gla_kernel(*refs, seq, n_heads, is_ctx, gpi):
    q_ref, k_ref, v_ref, og_ref, lr_ref, wgf_ref, wgb_ref, bgf_ref, bgb_ref, nw_ref = refs[:10]
    if is_ctx:
        o_ref, sf_ref, sb_ref = refs[-16:-13]
    else:
        s0f_ref, s0b_ref = refs[10:12]
        o_ref = refs[-14]
    (gf_scr, gb_scr, qef_scr, qeb_scr, o_scr, uf_scr, ub_scr, sinf_scr, sinb_scr,
     decf_scr, decb_scr, stf_scr, stb_scr) = refs[-13:]
    c = GLA_CHUNK
    grp = GLA_GROUP
    cpg = grp // c
    n_chunks = seq // c
    n_groups = seq // grp
    q_scale = GLA_DK ** -0.5

    lr = lr_ref[...].astype(BF16)
    gf_scr[...] = _log_sigmoid(_dot(lr, wgf_ref[0]) + bgf_ref[0]) / GATE_NORM
    gb_scr[...] = _log_sigmoid(_dot(lr, wgb_ref[0]) + bgb_ref[0]) / GATE_NORM

    ri = lax.broadcasted_iota(jnp.int32, (grp, grp), 0)
    ci = lax.broadcasted_iota(jnp.int32, (grp, grp), 1)
    same_chunk = (ri // c) == (ci // c)
    keep_f = same_chunk & (ri >= ci)
    keep_b = same_chunk & (ri <= ci)

    def cumsum(keep, g):
        tri = keep.astype(BF16)
        g_hi = g.astype(BF16)
        r1 = g - g_hi.astype(F32)
        g_mid = r1.astype(BF16)
        g_lo = (r1 - g_mid.astype(F32)).astype(BF16)
        return _dot(jnp.concatenate([tri, tri, tri], axis=1),
                    jnp.concatenate([g_hi, g_mid, g_lo], axis=0))

    dirs = ((gf_scr, keep_f, c - 1, qef_scr, uf_scr, decf_scr),
            (gb_scr, keep_b, 0, qeb_scr, ub_scr, decb_scr))

    hks = [slice(h * GLA_DK, (h + 1) * GLA_DK) for h in range(n_heads)]
    hvs = [slice(h * GLA_DV, (h + 1) * GLA_DV) for h in range(n_heads)]

    def group_body(it, carry):
        items = [(it * gpi + gi, h) for gi in range(gpi) for h in range(n_heads)]
        rows = [pl.ds(pl.multiple_of(r * grp, grp), grp) for r, _ in items]
        chains = [(x, d) for x in range(len(items)) for d in range(2)]
        qs = [q_ref[rows[x], hks[h]] * q_scale for x, (_, h) in enumerate(items)]
        ks = [k_ref[rows[x], hks[h]] for x, (_, h) in enumerate(items)]
        vs = [v_ref[rows[x], hvs[h]].astype(BF16) for x, (_, h) in enumerate(items)]
        bs = [cumsum(dirs[d][1], dirs[d][0][rows[x], hks[items[x][1]]]) for x, d in chains]
        ebs = [jnp.exp(b) for b in bs]
        embs = [jnp.exp(-b) for b in bs]
        qes = [(qs[x] * ebs[i]).astype(BF16) for i, (x, d) in enumerate(chains)]
        kes = [ks[x] * embs[i] for i, (x, d) in enumerate(chains)]
        for i, (x, d) in enumerate(chains):
            dirs[d][3][rows[x], hks[items[x][1]]] = qes[i]
        avals = [jnp.where(dirs[d][1], _dot_nt(qes[i], kes[i].astype(BF16)), 0.0)
                 for i, (x, d) in enumerate(chains)]
        decays = [[jnp.exp(bs[i][cc * c + dirs[d][2]:cc * c + dirs[d][2] + 1, :])
                   for cc in range(cpg)] for i, (x, d) in enumerate(chains)]
        kds = [[(kes[i][cc * c:(cc + 1) * c] * decays[i][cc]).astype(BF16) for cc in range(cpg)]
               for i in range(len(chains))]
        for i, (x, d) in enumerate(chains):
            r, h = items[x]
            for cc in range(cpg):
                n = r * cpg + cc
                dirs[d][4][h, n] = _dot_tn(vs[x][cc * c:(cc + 1) * c], kds[i][cc])
                dirs[d][5][h, n] = jnp.broadcast_to(decays[i][cc], (8, GLA_DK))
        for x, (_, h) in enumerate(items):
            a_sum = (avals[2 * x] + avals[2 * x + 1]).astype(BF16)
            o_scr[rows[x], hvs[h]] = _dot(a_sum, vs[x])
        return carry

    lax.fori_loop(0, n_groups // gpi, group_body, 0)

    for h in range(n_heads):
        if is_ctx:
            stf_scr[h] = jnp.zeros((GLA_DV, GLA_DK), F32)
            stb_scr[h] = jnp.zeros((GLA_DV, GLA_DK), F32)
        else:
            stf_scr[h] = s0f_ref[0, 0, h].T
            stb_scr[h] = s0b_ref[0, 0, h].T

    def scan_body(n, carry):
        nb = n_chunks - 1 - n
        for h in range(n_heads):
            s = stf_scr[h]
            sinf_scr[h, n] = s.astype(BF16)
            stf_scr[h] = decf_scr[h, n][0:1, :] * s + uf_scr[h, n]
            s = stb_scr[h]
            sinb_scr[h, nb] = s.astype(BF16)
            stb_scr[h] = decb_scr[h, nb][0:1, :] * s + ub_scr[h, nb]
        return carry

    lax.fori_loop(0, n_chunks, scan_body, 0)
    if is_ctx:
        for h in range(n_heads):
            sf_ref[0, 0, h] = stf_scr[h].T
            sb_ref[0, 0, h] = stb_scr[h].T

    nw = nw_ref[0]

    def finish(it, carry):
        items = [(it * gpi + gi, h) for gi in range(gpi) for h in range(n_heads)]
        rows = [pl.ds(pl.multiple_of(r * grp, grp), grp) for r, _ in items]
        inter = []
        for r, h in items:
            parts = []
            for cc in range(cpg):
                n = r * cpg + cc
                cr = pl.ds(pl.multiple_of(r * grp + cc * c, c), c)
                qe2 = jnp.concatenate([qef_scr[cr, hks[h]], qeb_scr[cr, hks[h]]], axis=1)
                s2 = jnp.concatenate([sinf_scr[h, n], sinb_scr[h, n]], axis=1)
                parts.append(_dot_nt(qe2, s2))
            inter.append(jnp.concatenate(parts, axis=0))
        os = [o_scr[rows[x], hvs[h]] + inter[x] for x, (_, h) in enumerate(items)]
        scales = [lax.rsqrt(jnp.mean(o * o, axis=-1, keepdims=True) + LN_EPS) for o in os]
        gates = [_silu(og_ref[rows[x], hvs[h]]) for x, (_, h) in enumerate(items)]
        for x, (_, h) in enumerate(items):
            o_ref[rows[x], hvs[h]] = (os[x] * scales[x] * nw * gates[x]).astype(BF16)
        return carry

    lax.fori_loop(0, n_groups // gpi, finish, 0)


def _gla(path, proj, proj_lr, wgf, wgb, bgf, bgb, norm_w, layer, *, s0=None, states=None):
    seq = path.seq
    n_heads = GLA_HEADS if path.is_ctx else 1
    n_chunks = seq // GLA_CHUNK
    n_groups = seq // GLA_GROUP

    def col_spec(width, col):
        w = n_heads * width
        return pl.BlockSpec((seq, w), lambda b, h: (b, col // w + h))

    head_w = pl.BlockSpec((1, LR_PAD, n_heads * GLA_DK), lambda b, h: (layer, 0, h))
    head_b = pl.BlockSpec((1, 1, n_heads * GLA_DK), lambda b, h: (layer, 0, h))
    state_spec = pl.BlockSpec((1, 1, n_heads, GLA_DK, GLA_DV), lambda b, h: (b, layer, h, 0, 0))
    in_specs = [
        col_spec(GLA_DK, COL_GQ), col_spec(GLA_DK, COL_GK), col_spec(GLA_DV, COL_GV),
        col_spec(GLA_DV, COL_GOG),
        pl.BlockSpec((seq, LR_PAD), lambda b, h: (b, 0)),
        head_w, head_w, head_b, head_b,
        pl.BlockSpec((1, 1, GLA_DV), lambda b, h: (layer, 0, 0)),
    ]
    args = [proj, proj, proj, proj, proj_lr, wgf, wgb, bgf, bgb, norm_w]
    o_shape = jax.ShapeDtypeStruct((path.rows, GLA_WIDTH), BF16)
    o_spec = pl.BlockSpec((seq, n_heads * GLA_DV), lambda b, h: (b, h))
    aliases = {}
    if path.is_ctx:
        state_shape = jax.ShapeDtypeStruct((BATCH, DEPTH, GLA_HEADS, GLA_DK, GLA_DV), F32)
        out_shape = (o_shape, state_shape, state_shape)
        out_specs = (o_spec, state_spec, state_spec)
        in_specs += [pl.BlockSpec(memory_space=pl.ANY), pl.BlockSpec(memory_space=pl.ANY)]
        args += list(states)
        aliases = {10: 1, 11: 2}
    else:
        in_specs += [state_spec, state_spec]
        args += list(s0)
        out_shape = (o_shape,)
        out_specs = (o_spec,)
    per_chunk = (n_heads, n_chunks, GLA_DV, GLA_DK)
    return pl.pallas_call(
        functools.partial(_gla_kernel, seq=seq, n_heads=n_heads, is_ctx=path.is_ctx,
                          gpi=min(n_groups, GLA_CHAINS // (2 * n_heads))),
        out_shape=out_shape,
        grid=(path.n_batch, GLA_HEADS // n_heads),
        in_specs=in_specs,
        out_specs=out_specs,
        input_output_aliases=aliases,
        scratch_shapes=[
            pltpu.VMEM((seq, n_heads * GLA_DK), F32), pltpu.VMEM((seq, n_heads * GLA_DK), F32),
            pltpu.VMEM((seq, n_heads * GLA_DK), BF16), pltpu.VMEM((seq, n_heads * GLA_DK), BF16),
            pltpu.VMEM((seq, n_heads * GLA_DV), F32),
            pltpu.VMEM(per_chunk, F32), pltpu.VMEM(per_chunk, F32),
            pltpu.VMEM(per_chunk, BF16), pltpu.VMEM(per_chunk, BF16),
            pltpu.VMEM((n_heads, n_chunks, 8, GLA_DK), F32),
            pltpu.VMEM((n_heads, n_chunks, 8, GLA_DK), F32),
            pltpu.VMEM((n_heads, GLA_DV, GLA_DK), F32), pltpu.VMEM((n_heads, GLA_DV, GLA_DK), F32),
        ],
        compiler_params=_params(("arbitrary", "arbitrary")),
        name="gla_" + path.name,
    )(*args)


def _outproj_kernel(oa_ref, og_ref, x_ref, g_ref, w_ref, lw_ref, lb_ref, o_ref):
    half = o_ref.shape[0] // 2
    rows = (slice(0, half), slice(half, 2 * half))
    fs = [_dot(oa_ref[r, :], w_ref[0, :ATTN_WIDTH, :]) + _dot(og_ref[r, :], w_ref[0, ATTN_WIDTH:, :])
          for r in rows]
    for r, f in zip(rows, fs):
        y = DEEPNORM_ALPHA * x_ref[r, :] + g_ref[0] * f
        o_ref[r, :] = _layer_norm(y, lw_ref[0], lb_ref[0])


def _out_proj(path, o_att, o_gla, x, mods, w_out, ln_w, ln_b, layer):
    tm = TM_OUT
    vec = _layer_vec_spec(layer, D_MODEL, 1)
    return pl.pallas_call(
        _outproj_kernel,
        out_shape=jax.ShapeDtypeStruct((path.rows, D_MODEL), F32),
        grid=(path.rows // tm,),
        in_specs=[
            pl.BlockSpec((tm, ATTN_WIDTH), lambda i: (i, 0)),
            pl.BlockSpec((tm, GLA_WIDTH), lambda i: (i, 0)),
            pl.BlockSpec((tm, D_MODEL), lambda i: (i, 0)),
            _mod_spec(path, layer, 2, tm),
            pl.BlockSpec((1, D_MODEL, D_MODEL), lambda i: (layer, 0, 0)),
            vec, vec,
        ],
        out_specs=pl.BlockSpec((tm, D_MODEL), lambda i: (i, 0)),
        compiler_params=_params(("arbitrary",)),
        name="out_proj_" + path.name,
    )(o_att, o_gla, x, mods, w_out, ln_w, ln_b)


def _ffn_kernel(*refs, tm, nf, seq, has_halo):
    if has_halo:
        x_ref, xp_ref, xn_ref = refs[:3]
        refs = refs[3:]
    else:
        x_ref = refs[0]
        refs = refs[1:]
    (sh_ref, sc_ref, g_ref, wa_ref, wg_ref, cwa_ref, cwg_ref, cba_ref, cbg_ref, wd_ref, lw_ref,
     lb_ref, o_ref, h_scr, ua0_scr, ug0_scr, ua1_scr, ug1_scr, act0_scr, act1_scr) = refs
    i = pl.program_id(0)
    j = pl.program_id(1)
    u_slots = ((ua0_scr, ug0_scr), (ua1_scr, ug1_scr))
    act_slots = (act0_scr, act1_scr)
    tf = wa_ref.shape[-1]
    hrows = HALO if has_halo else 0
    h_rows = tm + 2 * hrows
    seg = tm if has_halo else seq

    def u_row(r):
        return r if has_halo else GAP + r + (r // seg) * GAP

    up_n, up_m = tf // 2, h_rows // 2
    act_r, act_c = 128, 128
    down_m, down_n = tm // 2, D_MODEL // 4

    def up_piece(slot, which, n0, m0):
        w_ref = (wa_ref, wg_ref)[which]
        u = _dot(h_scr[m0:m0 + up_m, :], w_ref[0, :, n0:n0 + up_n])
        run_len = up_m if has_halo else seg
        for r in range(0, up_m, run_len):
            s0 = u_row(m0 + r)
            u_slots[slot][which][s0:s0 + run_len, n0:n0 + up_n] = u[r:r + run_len]

    def down_piece(slot, m0, n0):
        o_ref[m0:m0 + down_m, n0:n0 + down_n] += _dot(act_slots[slot][m0:m0 + down_m, :],
                                                      wd_ref[0, :, n0:n0 + down_n])

    def act_piece(slot, r0, c0):
        cols = slice(c0, c0 + act_c)
        s0 = u_row(r0 + hrows)

        def conv(u_ref, cw_ref, cb_ref):
            cw = cw_ref[0, :, cols]
            return (u_ref[s0 - 1:s0 - 1 + act_r, cols] * cw[0:1] + u_ref[s0:s0 + act_r, cols] * cw[1:2]
                    + u_ref[s0 + 1:s0 + 1 + act_r, cols] * cw[2:3] + cb_ref[0, :, cols])

        a = conv(u_slots[slot][0], cwa_ref, cba_ref)
        g = conv(u_slots[slot][1], cwg_ref, cbg_ref)
        act_slots[slot][r0:r0 + act_r, cols] = (_silu(g) * a).astype(BF16)

    def run(up=None, act=None, down=None):
        pieces = []

        def add(fn, arg_list):
            for k, a in enumerate(arg_list):
                pieces.append(((k + 0.5) / len(arg_list), len(pieces), fn, a))

        if up is not None:
            add(up_piece, [(up, w, n0, m0) for w in (0, 1) for n0 in range(0, tf, up_n)
                           for m0 in range(0, h_rows, up_m)])
        if act is not None:
            add(act_piece, [(act, r0, c0) for c0 in range(0, tf, act_c)
                            for r0 in range(0, tm, act_r)])
        if down is not None:
            add(down_piece, [(down, m0, n0) for n0 in range(0, D_MODEL, down_n)
                             for m0 in range(0, tm, down_m)])
        for _, _, fn, a in sorted(pieces, key=lambda t: t[:2]):
            fn(*a)

    @pl.when(j == 0)
    def _():
        sc = 1.0 + sc_ref[0]
        sh = sh_ref[0]
        h_scr[hrows:hrows + tm, :] = (x_ref[...] * sc + sh).astype(BF16)
        if has_halo:
            tile_in_seq = i % (seq // tm)
            h_scr[0:HALO, :] = jnp.where(tile_in_seq != 0, xp_ref[...] * sc + sh, 0.0).astype(BF16)
            h_scr[HALO + tm:2 * HALO + tm, :] = jnp.where(
                tile_in_seq != seq // tm - 1, xn_ref[...] * sc + sh, 0.0).astype(BF16)
        else:
            for pair in u_slots:
                for u_scr in pair:
                    for r in range(0, tm + 1, seg):
                        u_scr[u_row(r) - GAP:u_row(r), :] = jnp.zeros((GAP, tf), F32)
        o_ref[...] = jnp.zeros_like(o_ref)
        run(up=0)

    @pl.when(j == 1)
    def _():
        run(up=1, act=0)

    for parity in (0, 1):
        @pl.when((j >= 2) & (j < nf) & (j % 2 == parity))
        def _(parity=parity):
            run(up=parity, act=1 - parity, down=parity)

    @pl.when(j == nf)
    def _():
        run(act=(nf - 1) % 2, down=nf % 2)

    @pl.when(j == nf + 1)
    def _():
        run(down=(nf - 1) % 2)
        y = DEEPNORM_ALPHA * x_ref[...] + g_ref[0] * o_ref[...]
        o_ref[...] = _layer_norm(y, lw_ref[0], lb_ref[0])


def _ffn(path, x, mods, w_up, conv_w, conv_b, w_down, ln_w, ln_b, layer):
    tm, tf = TM_FFN, TF_FFN
    nf = D_FF // tf
    hb = tm // HALO
    n_halo_blocks = path.rows // HALO
    has_halo = path.seq > tm
    vec = _layer_vec_spec(layer, D_MODEL, 2)

    def chunk(j, lag):
        return jnp.clip(j - lag, 0, nf - 1)

    x_specs = [pl.BlockSpec((tm, D_MODEL), lambda i, j: (i, 0))]
    x_args = [x]
    if has_halo:
        x_specs += [
            pl.BlockSpec((HALO, D_MODEL), lambda i, j: (jnp.maximum(i * hb - 1, 0), 0)),
            pl.BlockSpec((HALO, D_MODEL),
                         lambda i, j: (jnp.minimum((i + 1) * hb, n_halo_blocks - 1), 0)),
        ]
        x_args += [x, x]
    h_rows = tm + 2 * HALO if has_halo else tm
    u_rows = tm + 2 * HALO if has_halo else GAP + (tm // path.seq) * (path.seq + GAP)
    u_scratch = pltpu.VMEM((u_rows, tf), F32)
    act_scratch = pltpu.VMEM((tm, tf), BF16)
    return pl.pallas_call(
        functools.partial(_ffn_kernel, tm=tm, nf=nf, seq=path.seq, has_halo=has_halo),
        out_shape=jax.ShapeDtypeStruct((path.rows, D_MODEL), F32),
        grid=(path.rows // tm, nf + 2),
        in_specs=x_specs + [
            _mod_spec(path, layer, 3, tm), _mod_spec(path, layer, 4, tm),
            _mod_spec(path, layer, 5, tm),
            pl.BlockSpec((1, D_MODEL, tf), lambda i, j: (layer, 0, chunk(j, 0))),
            pl.BlockSpec((1, D_MODEL, tf), lambda i, j: (layer, 0, nf + chunk(j, 0))),
            pl.BlockSpec((1, 3, tf), lambda i, j: (layer, 0, chunk(j, 1))),
            pl.BlockSpec((1, 3, tf), lambda i, j: (layer, 0, nf + chunk(j, 1))),
            pl.BlockSpec((1, 1, tf), lambda i, j: (layer, 0, chunk(j, 1))),
            pl.BlockSpec((1, 1, tf), lambda i, j: (layer, 0, nf + chunk(j, 1))),
            pl.BlockSpec((1, tf, D_MODEL), lambda i, j: (layer, chunk(j, 2), 0)),
            vec, vec,
        ],
        out_specs=pl.BlockSpec((tm, D_MODEL), lambda i, j: (i, 0), pipeline_mode=pl.Buffered(1)),
        scratch_shapes=[pltpu.VMEM((h_rows, D_MODEL), BF16),
                        u_scratch, u_scratch, u_scratch, u_scratch, act_scratch, act_scratch],
        compiler_params=_params(("arbitrary", "arbitrary")),
        name="ffn_" + path.name,
    )(*x_args, mods, mods, mods, w_up, w_up, conv_w, conv_w, conv_b, conv_b, w_down, ln_w, ln_b)


def _pad_gate(w, row0):
    return jnp.pad(w, ((0, 0), (row0, LR_PAD - GATE_RANK - row0), (0, 0))).astype(BF16)


def kernel(x_prompt, x_sample, cache_k, cache_v, state_gla_fwd, state_gla_bwd, c, c_ctx, w_ada,
           b_ada, w_in, attn_sink, w_gate_f, b_gate_f, w_gate_b, b_gate_b, gla_norm_w, w_out,
           ln1_w, ln1_b, w_up, conv_w, conv_b, w_down, ln2_w, ln2_b):
    cc = jnp.concatenate([c_ctx[None, :], c, jnp.zeros((MOD_ROWS - 1 - DEC_BATCH, D_MODEL), F32)], axis=0)
    mods = _ada(cc, w_ada, b_ada).reshape(DEPTH * MOD_ROWS * N_MODS, 1, D_MODEL)
    cos_t, sin_t = _rope_tables()

    w_main = jnp.concatenate([w_in[:, :, a:b] for a, b in W_IN_ORDER], axis=2).astype(BF16)
    w_lr = jnp.pad(w_in[:, :, PROJ_MAIN:], ((0, 0), (0, 0), (0, LR_PAD - 2 * GATE_RANK))).astype(BF16)
    w_up_b = w_up.astype(BF16)
    w_down_b = w_down.astype(BF16)
    w_out_b = w_out.astype(BF16)
    wgf = _pad_gate(w_gate_f, 0)
    wgb = _pad_gate(w_gate_b, GATE_RANK)
    bgf = b_gate_f[:, None, :]
    bgb = b_gate_b[:, None, :]
    norm_w = gla_norm_w[:, None, :]
    conv_b3 = conv_b[:, None, :]
    ln1_w3, ln1_b3, ln2_w3, ln2_b3 = (a[:, None, :] for a in (ln1_w, ln1_b, ln2_w, ln2_b))

    xs = {CTX: x_prompt.reshape(N_CTX, D_MODEL), LAT: x_sample.reshape(N_LAT, D_MODEL)}
    new_k, new_v = [], []
    states = (jnp.zeros((BATCH, DEPTH, GLA_HEADS, GLA_DK, GLA_DV), F32),) * 2
    for l in range(DEPTH):
        for path in (CTX, LAT):
            x = xs[path]
            proj, proj_lr = _in_proj(path, x, mods, w_main, w_lr, l)
            if path.is_ctx:
                o_att = _attn_ctx(proj, attn_sink, l)
                o_gla, sf, sb = _gla(path, proj, proj_lr, wgf, wgb, bgf, bgb, norm_w, l, states=states)
                states = (sf, sb)
                new_k.append(proj[:, COL_K:COL_K + KV_WIDTH].reshape(BATCH, SEQ, ATTN_KV_HEADS, HEAD_DIM))
                new_v.append(proj[:, COL_V:COL_V + KV_WIDTH].reshape(BATCH, SEQ, ATTN_KV_HEADS, HEAD_DIM))
            else:
                o_att = _attn_lat(proj, attn_sink, cache_k, cache_v, cos_t, sin_t, l)
                (o_gla,) = _gla(path, proj, proj_lr, wgf, wgb, bgf, bgb, norm_w, l,
                                s0=(state_gla_fwd, state_gla_bwd))
            x = _out_proj(path, o_att, o_gla, x, mods, w_out_b, ln1_w3, ln1_b3, l)
            xs[path] = _ffn(path, x, mods, w_up_b, conv_w, conv_b3, w_down_b, ln2_w3, ln2_b3, l)

    return (xs[CTX].reshape(BATCH, SEQ, D_MODEL), xs[LAT].reshape(DEC_BATCH, DEC_SEQ, D_MODEL),
            jnp.stack(new_k, axis=1), jnp.stack(new_v, axis=1), states[0], states[1])
```

```python
import functools

import jax
import jax.numpy as jnp
from jax import lax
from jax.experimental import pallas as pl
from jax.experimental.pallas import tpu as pltpu

F32 = jnp.float32
BF16 = jnp.bfloat16

D_MODEL = 2048
BATCH = 32
SEQ = 256
DEPTH = 2
DEC_BATCH = 4
DEC_SEQ = 2048
PAST_LEN = 256
GRID_W = 64
HEAD_DIM = 128
ATTN_WIDTH = 1024
ATTN_HEADS = 8
ATTN_KV_HEADS = 2
ATTN_GROUP = 4
KV_WIDTH = 256
WINDOW = 128
BLOCK = 128
GLA_WIDTH = 1024
GLA_HEADS = 4
GLA_DV = 256
GLA_DK = 128
GLA_KEY_WIDTH = 512
GATE_RANK = 16
GATE_NORM = 16.0
GLA_CHUNK = 64
GLA_GROUP = 256
GLA_CHAINS = 8
D_FF = 5632
ROPE_THETA = 10000.0
LN_EPS = 1e-5
NEG_INF = -1e30
DEEPNORM_ALPHA = (2 * DEPTH) ** 0.25

N_CTX = BATCH * SEQ
N_LAT = DEC_BATCH * DEC_SEQ
MOD_ROWS = 8
N_MODS = 6

PROJ_MAIN = 4608
COL_GV = ATTN_WIDTH
COL_GOG = COL_GV + GLA_WIDTH
COL_GQ = COL_GOG + GLA_WIDTH
COL_GK = COL_GQ + GLA_KEY_WIDTH
COL_K = COL_GK + GLA_KEY_WIDTH
COL_V = COL_K + KV_WIDTH
LR_PAD = 128
W_IN_ORDER = ((0, 1024), (2560, 3584), (3584, 4608), (1536, 2048), (2048, 2560), (1024, 1280),
              (1280, 1536))

VMEM_LIMIT = 56 * 1024 * 1024

TM_PROJ = 1024
TN_PROJ = 1536
TM_OUT = 512
OUT_ROW_BLOCK = 256
TM_FFN = 1024
TF_FFN = 512
HALO = 16
GAP = 8
TN_ADA = 1024


class Path:
    def __init__(self, name, n_batch, seq):
        self.name = name
        self.n_batch = n_batch
        self.seq = seq
        self.rows = n_batch * seq
        self.is_ctx = name == "ctx"

    def mod_row(self, i, tm):
        return 0 if self.is_ctx else 1 + i // (self.seq // tm)


CTX = Path("ctx", BATCH, SEQ)
LAT = Path("lat", DEC_BATCH, DEC_SEQ)


def _params(sem):
    return pltpu.CompilerParams(dimension_semantics=sem, vmem_limit_bytes=VMEM_LIMIT)


def _silu(x):
    return x / (1.0 + jnp.exp(-x))


def _layer_norm(y, w, b):
    mu = jnp.mean(y, axis=-1, keepdims=True)
    d = y - mu
    var = jnp.mean(d * d, axis=-1, keepdims=True)
    return d * lax.rsqrt(var + LN_EPS) * w + b


def _dot(a, b):
    return jnp.dot(a, b, preferred_element_type=F32)


def _dot_nt(a, b):
    return lax.dot_general(a, b, (((1,), (1,)), ((), ())), preferred_element_type=F32)


def _dot_tn(a, b):
    return lax.dot_general(a, b, (((0,), (0,)), ((), ())), preferred_element_type=F32)


def _mod_spec(path, layer, chunk, tm):
    def index_map(i, *_):
        return ((layer * MOD_ROWS + path.mod_row(i, tm)) * N_MODS + chunk, 0, 0)
    return pl.BlockSpec((1, 1, D_MODEL), index_map)


def _layer_vec_spec(layer, width, n_grid):
    if n_grid == 1:
        return pl.BlockSpec((1, 1, width), lambda i: (layer, 0, 0))
    return pl.BlockSpec((1, 1, width), lambda i, j: (layer, 0, 0))


def _ada_kernel(c_ref, w_ref, b_ref, o_ref):
    s = _silu(c_ref[...]).astype(BF16)
    o_ref[0] = _dot(s, w_ref[0].astype(BF16)) + b_ref[0]


def _ada(cc, w_ada, b_ada):
    n_out = N_MODS * D_MODEL
    return pl.pallas_call(
        _ada_kernel,
        out_shape=jax.ShapeDtypeStruct((DEPTH, MOD_ROWS, n_out), F32),
        grid=(DEPTH, n_out // TN_ADA),
        in_specs=[
            pl.BlockSpec((MOD_ROWS, D_MODEL), lambda l, j: (0, 0)),
            pl.BlockSpec((1, D_MODEL, TN_ADA), lambda l, j: (l, 0, j)),
            pl.BlockSpec((1, 1, TN_ADA), lambda l, j: (l, 0, j)),
        ],
        out_specs=pl.BlockSpec((1, MOD_ROWS, TN_ADA), lambda l, j: (l, 0, j)),
        compiler_params=_params(("arbitrary", "arbitrary")),
        name="ada",
    )(cc, w_ada, b_ada.reshape(DEPTH, 1, n_out))


def _inproj_kernel(x_ref, sh_ref, sc_ref, w_ref, wlr_ref, o_ref, olr_ref, h_scr):
    @pl.when(pl.program_id(1) == 0)
    def _():
        h = (x_ref[...] * (1.0 + sc_ref[0]) + sh_ref[0]).astype(BF16)
        h_scr[...] = h
        olr_ref[...] = _dot(h, wlr_ref[0])
    o_ref[...] = _dot(h_scr[...], w_ref[0])


def _in_proj(path, x, mods, w_main, w_lr, layer):
    tm, tn = TM_PROJ, TN_PROJ
    return pl.pallas_call(
        _inproj_kernel,
        out_shape=(jax.ShapeDtypeStruct((path.rows, PROJ_MAIN), F32),
                   jax.ShapeDtypeStruct((path.rows, LR_PAD), F32)),
        grid=(path.rows // tm, PROJ_MAIN // tn),
        in_specs=[
            pl.BlockSpec((tm, D_MODEL), lambda i, j: (i, 0)),
            _mod_spec(path, layer, 0, tm),
            _mod_spec(path, layer, 1, tm),
            pl.BlockSpec((1, D_MODEL, tn), lambda i, j: (layer, 0, j)),
            pl.BlockSpec((1, D_MODEL, LR_PAD), lambda i, j: (layer, 0, 0)),
        ],
        out_specs=(pl.BlockSpec((tm, tn), lambda i, j: (i, j)),
                   pl.BlockSpec((tm, LR_PAD), lambda i, j: (i, 0))),
        scratch_shapes=[pltpu.VMEM((tm, D_MODEL), BF16)],
        compiler_params=_params(("arbitrary", "arbitrary")),
        name="in_proj_" + path.name,
    )(x, mods, mods, w_main, w_lr)


def _sink_softmax_pv(sinks, scores, values):
    heads = range(len(sinks))
    ms = [functools.reduce(jnp.maximum, [jnp.max(s, axis=-1, keepdims=True) for s in scores[h]],
                           sinks[h]) for h in heads]
    es = [[jnp.exp(s - ms[h]) for s in scores[h]] for h in heads]
    dens = [functools.reduce(lambda a, e: a + jnp.sum(e, axis=-1, keepdims=True), es[h],
                             jnp.exp(sinks[h] - ms[h])) for h in heads]
    invs = [1.0 / d for d in dens]
    ps = [[(e * invs[h]).astype(BF16) for e in es[h]] for h in heads]
    return [functools.reduce(lambda a, b: a + b, [_dot(p, v) for p, v in zip(ps[h], values[h])])
            for h in heads]


def _attn_ctx_kernel(sink_ref, q_ref, k_ref, v_ref, o_ref, *, layer):
    scale = HEAD_DIM ** -0.5
    heads = range(ATTN_HEADS)
    ks = [k_ref[:, kh * HEAD_DIM:(kh + 1) * HEAD_DIM].astype(BF16) for kh in range(ATTN_KV_HEADS)]
    vs = [v_ref[:, kh * HEAD_DIM:(kh + 1) * HEAD_DIM].astype(BF16) for kh in range(ATTN_KV_HEADS)]
    scores = [[_dot_nt(q_ref[:, h * HEAD_DIM:(h + 1) * HEAD_DIM].astype(BF16),
                       ks[h // ATTN_GROUP]) * scale] for h in heads]
    outs = _sink_softmax_pv([sink_ref[layer, h] for h in heads], scores,
                            [[vs[h // ATTN_GROUP]] for h in heads])
    for h in heads:
        o_ref[:, h * HEAD_DIM:(h + 1) * HEAD_DIM] = outs[h].astype(BF16)


def _attn_ctx(proj, sink, layer):
    return pl.pallas_call(
        functools.partial(_attn_ctx_kernel, layer=layer),
        out_shape=jax.ShapeDtypeStruct((N_CTX, ATTN_WIDTH), BF16),
        grid=(BATCH,),
        in_specs=[
            pl.BlockSpec(memory_space=pltpu.SMEM),
            pl.BlockSpec((SEQ, ATTN_WIDTH), lambda b: (b, 0)),
            pl.BlockSpec((SEQ, KV_WIDTH), lambda b: (b, COL_K // KV_WIDTH)),
            pl.BlockSpec((SEQ, KV_WIDTH), lambda b: (b, COL_V // KV_WIDTH)),
        ],
        out_specs=pl.BlockSpec((SEQ, ATTN_WIDTH), lambda b: (b, 0)),
        compiler_params=_params(("arbitrary",)),
        name="attn_ctx",
    )(sink, proj, proj, proj)


def _attn_lat_kernel(sink_ref, q_ref, k_ref, v_ref, ck_ref, cv_ref, cos_ref, sin_ref, o_ref, *,
                     layer):
    scale = HEAD_DIM ** -0.5
    win = 3 * BLOCK
    i = pl.program_id(1)
    q0 = pl.multiple_of(i * BLOCK, BLOCK)
    start = pl.multiple_of(jnp.clip((i - 1) * BLOCK, 0, DEC_SEQ - win), BLOCK)
    cos_q, sin_q = cos_ref[pl.ds(q0, BLOCK), :], sin_ref[pl.ds(q0, BLOCK), :]
    cos_k, sin_k = cos_ref[pl.ds(start, win), :], sin_ref[pl.ds(start, win), :]
    lane = lax.broadcasted_iota(jnp.int32, (1, HEAD_DIM), 1)
    low = (lane % (HEAD_DIM // 2)) < (HEAD_DIM // 4)

    def rope(x, c, s):
        partner = jnp.where(low, pltpu.roll(x, HEAD_DIM - HEAD_DIM // 4, 1),
                            pltpu.roll(x, HEAD_DIM // 4, 1))
        return x * c + partner * s

    qpos = q0 + lax.broadcasted_iota(jnp.int32, (BLOCK, 1), 0)
    kpos = start + lax.broadcasted_iota(jnp.int32, (1, win), 1)
    valid = jnp.abs(qpos - kpos) <= WINDOW

    heads = range(ATTN_HEADS)
    kv_cols = [slice(kh * HEAD_DIM, (kh + 1) * HEAD_DIM) for kh in range(ATTN_KV_HEADS)]
    kws = [rope(k_ref[pl.ds(start, win), c], cos_k, sin_k).astype(BF16) for c in kv_cols]
    vws = [v_ref[pl.ds(start, win), c].astype(BF16) for c in kv_cols]
    cks = [ck_ref[0, 0, :, c].astype(BF16) for c in kv_cols]
    cvs = [cv_ref[0, 0, :, c].astype(BF16) for c in kv_cols]
    qs = [rope(q_ref[:, h * HEAD_DIM:(h + 1) * HEAD_DIM], cos_q, sin_q).astype(BF16) for h in heads]
    scores = [[jnp.where(valid, _dot_nt(qs[h], kws[h // ATTN_GROUP]) * scale, NEG_INF),
               _dot_nt(qs[h], cks[h // ATTN_GROUP]) * scale] for h in heads]
    outs = _sink_softmax_pv([sink_ref[layer, h] for h in heads], scores,
                            [[vws[h // ATTN_GROUP], cvs[h // ATTN_GROUP]] for h in heads])
    for h in heads:
        o_ref[:, h * HEAD_DIM:(h + 1) * HEAD_DIM] = outs[h].astype(BF16)


def _attn_lat(proj, sink, cache_k, cache_v, cos_t, sin_t, layer):
    nb = DEC_SEQ // BLOCK
    cache_spec = pl.BlockSpec((1, 1, PAST_LEN, KV_WIDTH), lambda b, i: (b, layer, 0, 0))
    table_spec = pl.BlockSpec((DEC_SEQ, HEAD_DIM), lambda b, i: (0, 0))
    return pl.pallas_call(
        functools.partial(_attn_lat_kernel, layer=layer),
        out_shape=jax.ShapeDtypeStruct((N_LAT, ATTN_WIDTH), BF16),
        grid=(DEC_BATCH, nb),
        in_specs=[
            pl.BlockSpec(memory_space=pltpu.SMEM),
            pl.BlockSpec((BLOCK, ATTN_WIDTH), lambda b, i: (b * nb + i, 0)),
            pl.BlockSpec((DEC_SEQ, KV_WIDTH), lambda b, i: (b, COL_K // KV_WIDTH)),
            pl.BlockSpec((DEC_SEQ, KV_WIDTH), lambda b, i: (b, COL_V // KV_WIDTH)),
            cache_spec, cache_spec, table_spec, table_spec,
        ],
        out_specs=pl.BlockSpec((BLOCK, ATTN_WIDTH), lambda b, i: (b * nb + i, 0)),
        compiler_params=_params(("arbitrary", "arbitrary")),
        name="attn_lat",
    )(sink, proj, proj, proj,
      cache_k.reshape(DEC_BATCH, DEPTH, PAST_LEN, KV_WIDTH),
      cache_v.reshape(DEC_BATCH, DEPTH, PAST_LEN, KV_WIDTH), cos_t, sin_t)


def _rope_tables():
    half = HEAD_DIM // 2
    n_freq = half // 2
    t = jnp.arange(DEC_SEQ)
    row = (t // GRID_W).astype(F32)
    col = (t % GRID_W).astype(F32)
    freqs = ROPE_THETA ** (-jnp.arange(n_freq, dtype=F32) / n_freq)
    tabs = []
    for pos in (row, col):
        ang = pos[:, None] * freqs[None, :]
        tabs.append((jnp.cos(ang), jnp.sin(ang)))
    cos_t = jnp.concatenate([tabs[0][0], tabs[0][0], tabs[1][0], tabs[1][0]], axis=-1)
    sin_t = jnp.concatenate([-tabs[0][1], tabs[0][1], -tabs[1][1], tabs[1][1]], axis=-1)
    return cos_t, sin_t


def _log_sigmoid(x):
    return jnp.minimum(x, 0.0) - jnp.log1p(jnp.exp(-jnp.abs(x)))


def _gla_kernel(*refs, seq, n_heads, is_ctx, gpi):
    q_ref, k_ref, v_ref, og_ref, lr_ref, wgf_ref, wgb_ref, bgf_ref, bgb_ref, nw_ref = refs[:10]
    if is_ctx:
        o_ref, sf_ref, sb_ref = refs[-16:-13]
    else:
        s0f_ref, s0b_ref = refs[10:12]
        o_ref = refs[-14]
    (gf_scr, gb_scr, qef_scr, qeb_scr, o_scr, uf_scr, ub_scr, sinf_scr, sinb_scr,
     decf_scr, decb_scr, stf_scr, stb_scr) = refs[-13:]
    c = GLA_CHUNK
    grp = GLA_GROUP
    cpg = grp // c
    n_chunks = seq // c
    n_groups = seq // grp
    q_scale = GLA_DK ** -0.5

    lr = lr_ref[...].astype(BF16)
    gf_scr[...] = _log_sigmoid(_dot(lr, wgf_ref[0]) + bgf_ref[0]) / GATE_NORM
    gb_scr[...] = _log_sigmoid(_dot(lr, wgb_ref[0]) + bgb_ref[0]) / GATE_NORM

    ri = lax.broadcasted_iota(jnp.int32, (grp, grp), 0)
    ci = lax.broadcasted_iota(jnp.int32, (grp, grp), 1)
    same_chunk = (ri // c) == (ci // c)
    keep_f = same_chunk & (ri >= ci)
    keep_b = same_chunk & (ri <= ci)

    def cumsum(keep, g):
        tri = keep.astype(BF16)
        g_hi = g.astype(BF16)
        r1 = g - g_hi.astype(F32)
        g_mid = r1.astype(BF16)
        g_lo = (r1 - g_mid.astype(F32)).astype(BF16)
        return _dot(jnp.concatenate([tri, tri, tri], axis=1),
                    jnp.concatenate([g_hi, g_mid, g_lo], axis=0))

    dirs = ((gf_scr, keep_f, c - 1, qef_scr, uf_scr, decf_scr),
            (gb_scr, keep_b, 0, qeb_scr, ub_scr, decb_scr))

    hks = [slice(h * GLA_DK, (h + 1) * GLA_DK) for h in range(n_heads)]
    hvs = [slice(h * GLA_DV, (h + 1) * GLA_DV) for h in range(n_heads)]

    def group_body(it, carry):
        items = [(it * gpi + gi, h) for gi in range(gpi) for h in range(n_heads)]
        rows = [pl.ds(pl.multiple_of(r * grp, grp), grp) for r, _ in items]
        chains = [(x, d) for x in range(len(items)) for d in range(2)]
        qs = [q_ref[rows[x], hks[h]] * q_scale for x, (_, h) in enumerate(items)]
        ks = [k_ref[rows[x], hks[h]] for x, (_, h) in enumerate(items)]
        vs = [v_ref[rows[x], hvs[h]].astype(BF16) for x, (_, h) in enumerate(items)]
        bs = [cumsum(dirs[d][1], dirs[d][0][rows[x], hks[items[x][1]]]) for x, d in chains]
        ebs = [jnp.exp(b) for b in bs]
        embs = [jnp.exp(-b) for b in bs]
        qes = [(qs[x] * ebs[i]).astype(BF16) for i, (x, d) in enumerate(chains)]
        kes = [ks[x] * embs[i] for i, (x, d) in enumerate(chains)]
        for i, (x, d) in enumerate(chains):
            dirs[d][3][rows[x], hks[items[x][1]]] = qes[i]
        avals = [jnp.where(dirs[d][1], _dot_nt(qes[i], kes[i].astype(BF16)), 0.0)
                 for i, (x, d) in enumerate(chains)]
        decays = [[jnp.exp(bs[i][cc * c + dirs[d][2]:cc * c + dirs[d][2] + 1, :])
                   for cc in range(cpg)] for i, (x, d) in enumerate(chains)]
        kds = [[(kes[i][cc * c:(cc + 1) * c] * decays[i][cc]).astype(BF16) for cc in range(cpg)]
               for i in range(len(chains))]
        for i, (x, d) in enumerate(chains):
            r, h = items[x]
            for cc in range(cpg):
                n = r * cpg + cc
                dirs[d][4][h, n] = _dot_tn(vs[x][cc * c:(cc + 1) * c], kds[i][cc])
                dirs[d][5][h, n] = jnp.broadcast_to(decays[i][cc], (8, GLA_DK))
        for x, (_, h) in enumerate(items):
            a_sum = (avals[2 * x] + avals[2 * x + 1]).astype(BF16)
            o_scr[rows[x], hvs[h]] = _dot(a_sum, vs[x])
        return carry

    lax.fori_loop(0, n_groups // gpi, group_body, 0)

    for h in range(n_heads):
        if is_ctx:
            stf_scr[h] = jnp.zeros((GLA_DV, GLA_DK), F32)
            stb_scr[h] = jnp.zeros((GLA_DV, GLA_DK), F32)
        else:
            stf_scr[h] = s0f_ref[0, 0, h].T
            stb_scr[h] = s0b_ref[0, 0, h].T

    def scan_body(n, carry):
        nb = n_chunks - 1 - n
        for h in range(n_heads):
            s = stf_scr[h]
            sinf_scr[h, n] = s.astype(BF16)
            stf_scr[h] = decf_scr[h, n][0:1, :] * s + uf_scr[h, n]
            s = stb_scr[h]
            sinb_scr[h, nb] = s.astype(BF16)
            stb_scr[h] = decb_scr[h, nb][0:1, :] * s + ub_scr[h, nb]
        return carry

    lax.fori_loop(0, n_chunks, scan_body, 0)
    if is_ctx:
        for h in range(n_heads):
            sf_ref[0, 0, h] = stf_scr[h].T
            sb_ref[0, 0, h] = stb_scr[h].T

    nw = nw_ref[0]

    def finish(it, carry):
        items = [(it * gpi + gi, h) for gi in range(gpi) for h in range(n_heads)]
        rows = [pl.ds(pl.multiple_of(r * grp, grp), grp) for r, _ in items]
        inter = []
        for r, h in items:
            parts = []
            for cc in range(cpg):
                n = r * cpg + cc
                cr = pl.ds(pl.multiple_of(r * grp + cc * c, c), c)
                qe2 = jnp.concatenate([qef_scr[cr, hks[h]], qeb_scr[cr, hks[h]]], axis=1)
                s2 = jnp.concatenate([sinf_scr[h, n], sinb_scr[h, n]], axis=1)
                parts.append(_dot_nt(qe2, s2))
            inter.append(jnp.concatenate(parts, axis=0))
        os = [o_scr[rows[x], hvs[h]] + inter[x] for x, (_, h) in enumerate(items)]
        scales = [lax.rsqrt(jnp.mean(o * o, axis=-1, keepdims=True) + LN_EPS) for o in os]
        gates = [_silu(og_ref[rows[x], hvs[h]]) for x, (_, h) in enumerate(items)]
        for x, (_, h) in enumerate(items):
            o_ref[rows[x], hvs[h]] = (os[x] * scales[x] * nw * gates[x]).astype(BF16)
        return carry

    lax.fori_loop(0, n_groups // gpi, finish, 0)


def _gla(path, proj, proj_lr, wgf, wgb, bgf, bgb, norm_w, layer, *, s0=None, states=None):
    seq = path.seq
    n_heads = GLA_HEADS if path.is_ctx else 1
    n_chunks = seq // GLA_CHUNK
    n_groups = seq // GLA_GROUP

    def col_spec(width, col):
        w = n_heads * width
        return pl.BlockSpec((seq, w), lambda b, h: (b, col // w + h))

    head_w = pl.BlockSpec((1, LR_PAD, n_heads * GLA_DK), lambda b, h: (layer, 0, h))
    head_b = pl.BlockSpec((1, 1, n_heads * GLA_DK), lambda b, h: (layer, 0, h))
    state_spec = pl.BlockSpec((1, 1, n_heads, GLA_DK, GLA_DV), lambda b, h: (b, layer, h, 0, 0))
    in_specs = [
        col_spec(GLA_DK, COL_GQ), col_spec(GLA_DK, COL_GK), col_spec(GLA_DV, COL_GV),
        col_spec(GLA_DV, COL_GOG),
        pl.BlockSpec((seq, LR_PAD), lambda b, h: (b, 0)),
        head_w, head_w, head_b, head_b,
        pl.BlockSpec((1, 1, GLA_DV), lambda b, h: (layer, 0, 0)),
    ]
    args = [proj, proj, proj, proj, proj_lr, wgf, wgb, bgf, bgb, norm_w]
    o_shape = jax.ShapeDtypeStruct((path.rows, GLA_WIDTH), BF16)
    o_spec = pl.BlockSpec((seq, n_heads * GLA_DV), lambda b, h: (b, h))
    aliases = {}
    if path.is_ctx:
        state_shape = jax.ShapeDtypeStruct((BATCH, DEPTH, GLA_HEADS, GLA_DK, GLA_DV), F32)
        out_shape = (o_shape, state_shape, state_shape)
        out_specs = (o_spec, state_spec, state_spec)
        in_specs += [pl.BlockSpec(memory_space=pl.ANY), pl.BlockSpec(memory_space=pl.ANY)]
        args += list(states)
        aliases = {10: 1, 11: 2}
    else:
        in_specs += [state_spec, state_spec]
        args += list(s0)
        out_shape = (o_shape,)
        out_specs = (o_spec,)
    per_chunk = (n_heads, n_chunks, GLA_DV, GLA_DK)
    return pl.pallas_call(
        functools.partial(_gla_kernel, seq=seq, n_heads=n_heads, is_ctx=path.is_ctx,
                          gpi=min(n_groups, GLA_CHAINS // (2 * n_heads))),
        out_shape=out_shape,
        grid=(path.n_batch, GLA_HEADS // n_heads),
        in_specs=in_specs,
        out_specs=out_specs,
        input_output_aliases=aliases,
        scratch_shapes=[
            pltpu.VMEM((seq, n_heads * GLA_DK), F32), pltpu.VMEM((seq, n_heads * GLA_DK), F32),
            pltpu.VMEM((seq, n_heads * GLA_DK), BF16), pltpu.VMEM((seq, n_heads * GLA_DK), BF16),
            pltpu.VMEM((seq, n_heads * GLA_DV), F32),
            pltpu.VMEM(per_chunk, F32), pltpu.VMEM(per_chunk, F32),
            pltpu.VMEM(per_chunk, BF16), pltpu.VMEM(per_chunk, BF16),
            pltpu.VMEM((n_heads, n_chunks, 8, GLA_DK), F32),
            pltpu.VMEM((n_heads, n_chunks, 8, GLA_DK), F32),
            pltpu.VMEM((n_heads, GLA_DV, GLA_DK), F32), pltpu.VMEM((n_heads, GLA_DV, GLA_DK), F32),
        ],
        compiler_params=_params(("arbitrary", "arbitrary")),
        name="gla_" + path.name,
    )(*args)


def _outproj_kernel(oa_ref, og_ref, x_ref, g_ref, w_ref, lw_ref, lb_ref, o_ref):
    rb = OUT_ROW_BLOCK
    blocks = [slice(r0, r0 + rb) for r0 in range(0, o_ref.shape[0], rb)]

    def matmul(r):
        return (_dot(oa_ref[r, :], w_ref[0, :ATTN_WIDTH, :])
                + _dot(og_ref[r, :], w_ref[0, ATTN_WIDTH:, :]))

    def norm(r, f):
        y = DEEPNORM_ALPHA * x_ref[r, :] + g_ref[0] * f
        o_ref[r, :] = _layer_norm(y, lw_ref[0], lb_ref[0])

    f_prev = matmul(blocks[0])
    for k in range(1, len(blocks)):
        f_next = matmul(blocks[k])
        norm(blocks[k - 1], f_prev)
        f_prev = f_next
    norm(blocks[-1], f_prev)


def _out_proj(path, o_att, o_gla, x, mods, w_out, ln_w, ln_b, layer):
    tm = TM_OUT
    vec = _layer_vec_spec(layer, D_MODEL, 1)
    return pl.pallas_call(
        _outproj_kernel,
        out_shape=jax.ShapeDtypeStruct((path.rows, D_MODEL), F32),
        grid=(path.rows // tm,),
        in_specs=[
            pl.BlockSpec((tm, ATTN_WIDTH), lambda i: (i, 0)),
            pl.BlockSpec((tm, GLA_WIDTH), lambda i: (i, 0)),
            pl.BlockSpec((tm, D_MODEL), lambda i: (i, 0)),
            _mod_spec(path, layer, 2, tm),
            pl.BlockSpec((1, D_MODEL, D_MODEL), lambda i: (layer, 0, 0)),
            vec, vec,
        ],
        out_specs=pl.BlockSpec((tm, D_MODEL), lambda i: (i, 0)),
        compiler_params=_params(("arbitrary",)),
        name="out_proj_" + path.name,
    )(o_att, o_gla, x, mods, w_out, ln_w, ln_b)


def _ffn_kernel(*refs, tm, nf, seq, has_halo):
    if has_halo:
        x_ref, xp_ref, xn_ref = refs[:3]
        refs = refs[3:]
    else:
        x_ref = refs[0]
        refs = refs[1:]
    (sh_ref, sc_ref, g_ref, wa_ref, wg_ref, cp_ref, wd_ref, lw_ref,
     lb_ref, o_ref, h_scr, ua0_scr, ug0_scr, ua1_scr, ug1_scr, act0_scr, act1_scr) = refs
    i = pl.program_id(0)
    j = pl.program_id(1)
    u_slots = ((ua0_scr, ug0_scr), (ua1_scr, ug1_scr))
    act_slots = (act0_scr, act1_scr)
    tf = wa_ref.shape[-1]
    hrows = HALO if has_halo else 0
    h_rows = tm + 2 * hrows
    seg = tm if has_halo else seq

    def u_row(r):
        return r if has_halo else GAP + r + (r // seg) * GAP

    up_n, up_m = tf // 2, h_rows // 2
    act_r, act_c = 128, 128
    down_m, down_n = tm // 2, D_MODEL // 4

    def up_piece(slot, which, n0, m0):
        w_ref = (wa_ref, wg_ref)[which]
        u = _dot(h_scr[m0:m0 + up_m, :], w_ref[0, :, n0:n0 + up_n])
        run_len = up_m if has_halo else seg
        for r in range(0, up_m, run_len):
            s0 = u_row(m0 + r)
            u_slots[slot][which][s0:s0 + run_len, n0:n0 + up_n] = u[r:r + run_len]

    def down_piece(slot, m0, n0):
        o_ref[m0:m0 + down_m, n0:n0 + down_n] += _dot(act_slots[slot][m0:m0 + down_m, :],
                                                      wd_ref[0, :, n0:n0 + down_n])

    def act_piece(slot, r0, c0):
        cols = slice(c0, c0 + act_c)
        s0 = u_row(r0 + hrows)

        def conv(u_ref, which):
            cp = cp_ref[0, 0, :, which * tf + c0:which * tf + c0 + act_c]
            return (u_ref[s0 - 1:s0 - 1 + act_r, cols] * cp[0:1] + u_ref[s0:s0 + act_r, cols] * cp[1:2]
                    + u_ref[s0 + 1:s0 + 1 + act_r, cols] * cp[2:3] + cp[3:4])

        a = conv(u_slots[slot][0], 0)
        g = conv(u_slots[slot][1], 1)
        act_slots[slot][r0:r0 + act_r, cols] = (_silu(g) * a).astype(BF16)

    def run(up=None, act=None, down=None):
        pieces = []

        def add(fn, arg_list):
            for k, a in enumerate(arg_list):
                pieces.append(((k + 0.5) / len(arg_list), len(pieces), fn, a))

        if up is not None:
            add(up_piece, [(up, w, n0, m0) for w in (0, 1) for n0 in range(0, tf, up_n)
                           for m0 in range(0, h_rows, up_m)])
        if act is not None:
            add(act_piece, [(act, r0, c0) for c0 in range(0, tf, act_c)
                            for r0 in range(0, tm, act_r)])
        if down is not None:
            add(down_piece, [(down, m0, n0) for n0 in range(0, D_MODEL, down_n)
                             for m0 in range(0, tm, down_m)])
        for _, _, fn, a in sorted(pieces, key=lambda t: t[:2]):
            fn(*a)

    @pl.when(j == 0)
    def _():
        sc = 1.0 + sc_ref[0]
        sh = sh_ref[0]

        def build_h(part):
            xr = slice(part * (tm // 2), (part + 1) * (tm // 2))
            h_scr[hrows + xr.start:hrows + xr.stop, :] = (x_ref[xr, :] * sc + sh).astype(BF16)
            if has_halo:
                tile_in_seq = i % (seq // tm)
                if part == 0:
                    h_scr[0:HALO, :] = jnp.where(tile_in_seq != 0, xp_ref[...] * sc + sh,
                                                 0.0).astype(BF16)
                else:
                    h_scr[HALO + tm:2 * HALO + tm, :] = jnp.where(
                        tile_in_seq != seq // tm - 1, xn_ref[...] * sc + sh, 0.0).astype(BF16)

        if not has_halo:
            for pair in u_slots:
                for u_scr in pair:
                    for r in range(0, tm + 1, seg):
                        u_scr[u_row(r) - GAP:u_row(r), :] = jnp.zeros((GAP, tf), F32)
        build_h(0)
        zr = tm // 8
        pieces = [(0, w, n0, m0) for m0 in range(0, h_rows, up_m) for w in (0, 1)
                  for n0 in range(0, tf, up_n)]
        for k, a in enumerate(pieces):
            up_piece(*a)
            if k == 0:
                build_h(1)
            o_ref[k * zr:(k + 1) * zr, :] = jnp.zeros((zr, D_MODEL), F32)

    @pl.when(j == 1)
    def _():
        run(up=1, act=0)

    for parity in (0, 1):
        @pl.when((j >= 2) & (j < nf) & (j % 2 == parity))
        def _(parity=parity):
            run(up=parity, act=1 - parity, down=parity)

    @pl.when(j == nf)
    def _():
        run(act=(nf - 1) % 2, down=nf % 2)

    @pl.when(j == nf + 1)
    def _():
        for m0 in range(0, tm, down_m):
            for n0 in range(0, D_MODEL, down_n):
                down_piece((nf - 1) % 2, m0, n0)
        for m0 in range(0, tm, down_m):
            rows = slice(m0, m0 + down_m)
            y = DEEPNORM_ALPHA * x_ref[rows, :] + g_ref[0] * o_ref[rows, :]
            o_ref[rows, :] = _layer_norm(y, lw_ref[0], lb_ref[0])


def _pack_conv_params(conv_w, conv_b):
    nf = D_FF // TF_FFN
    cw = conv_w.reshape(DEPTH, 3, 2, nf, TF_FFN).transpose(0, 3, 1, 2, 4).reshape(DEPTH, nf, 3, 2 * TF_FFN)
    cb = conv_b.reshape(DEPTH, 1, 2, nf, TF_FFN).transpose(0, 3, 1, 2, 4).reshape(DEPTH, nf, 1, 2 * TF_FFN)
    return jnp.concatenate([cw, cb, jnp.zeros((DEPTH, nf, 4, 2 * TF_FFN), F32)], axis=2)


def _ffn(path, x, mods, w_up, conv_p, w_down, ln_w, ln_b, layer):
    tm, tf = TM_FFN, TF_FFN
    nf = D_FF // tf
    hb = tm // HALO
    n_halo_blocks = path.rows // HALO
    has_halo = path.seq > tm
    vec = _layer_vec_spec(layer, D_MODEL, 2)

    def chunk(j, lag):
        return jnp.clip(j - lag, 0, nf - 1)

    x_specs = [pl.BlockSpec((tm, D_MODEL), lambda i, j: (i, 0))]
    x_args = [x]
    if has_halo:
        x_specs += [
            pl.BlockSpec((HALO, D_MODEL), lambda i, j: (jnp.maximum(i * hb - 1, 0), 0)),
            pl.BlockSpec((HALO, D_MODEL),
                         lambda i, j: (jnp.minimum((i + 1) * hb, n_halo_blocks - 1), 0)),
        ]
        x_args += [x, x]
    h_rows = tm + 2 * HALO if has_halo else tm
    u_rows = tm + 2 * HALO if has_halo else GAP + (tm // path.seq) * (path.seq + GAP)
    u_scratch = pltpu.VMEM((u_rows, tf), F32)
    act_scratch = pltpu.VMEM((tm, tf), BF16)
    return pl.pallas_call(
        functools.partial(_ffn_kernel, tm=tm, nf=nf, seq=path.seq, has_halo=has_halo),
        out_shape=jax.ShapeDtypeStruct((path.rows, D_MODEL), F32),
        grid=(path.rows // tm, nf + 2),
        in_specs=x_specs + [
            _mod_spec(path, layer, 3, tm), _mod_spec(path, layer, 4, tm),
            _mod_spec(path, layer, 5, tm),
            pl.BlockSpec((1, D_MODEL, tf), lambda i, j: (layer, 0, chunk(j, 0))),
            pl.BlockSpec((1, D_MODEL, tf), lambda i, j: (layer, 0, nf + chunk(j, 0))),
            pl.BlockSpec((1, 1, 8, 2 * tf), lambda i, j: (layer, chunk(j, 1), 0, 0)),
            pl.BlockSpec((1, tf, D_MODEL), lambda i, j: (layer, chunk(j, 2), 0)),
            vec, vec,
        ],
        out_specs=pl.BlockSpec((tm, D_MODEL), lambda i, j: (i, 0), pipeline_mode=pl.Buffered(1)),
        scratch_shapes=[pltpu.VMEM((h_rows, D_MODEL), BF16),
                        u_scratch, u_scratch, u_scratch, u_scratch, act_scratch, act_scratch],
        compiler_params=_params(("arbitrary", "arbitrary")),
        name="ffn_" + path.name,
    )(*x_args, mods, mods, mods, w_up, w_up, conv_p, w_down, ln_w, ln_b)


def _pad_gate(w, row0):
    return jnp.pad(w, ((0, 0), (row0, LR_PAD - GATE_RANK - row0), (0, 0))).astype(BF16)


def kernel(x_prompt, x_sample, cache_k, cache_v, state_gla_fwd, state_gla_bwd, c, c_ctx, w_ada,
           b_ada, w_in, attn_sink, w_gate_f, b_gate_f, w_gate_b, b_gate_b, gla_norm_w, w_out,
           ln1_w, ln1_b, w_up, conv_w, conv_b, w_down, ln2_w, ln2_b):
    cc = jnp.concatenate([c_ctx[None, :], c, jnp.zeros((MOD_ROWS - 1 - DEC_BATCH, D_MODEL), F32)], axis=0)
    mods = _ada(cc, w_ada, b_ada).reshape(DEPTH * MOD_ROWS * N_MODS, 1, D_MODEL)
    cos_t, sin_t = _rope_tables()

    w_main = jnp.concatenate([w_in[:, :, a:b] for a, b in W_IN_ORDER], axis=2).astype(BF16)
    w_lr = jnp.pad(w_in[:, :, PROJ_MAIN:], ((0, 0), (0, 0), (0, LR_PAD - 2 * GATE_RANK))).astype(BF16)
    w_up_b = w_up.astype(BF16)
    w_down_b = w_down.astype(BF16)
    w_out_b = w_out.astype(BF16)
    wgf = _pad_gate(w_gate_f, 0)
    wgb = _pad_gate(w_gate_b, GATE_RANK)
    bgf = b_gate_f[:, None, :]
    bgb = b_gate_b[:, None, :]
    norm_w = gla_norm_w[:, None, :]
    conv_p = _pack_conv_params(conv_w, conv_b)
    ln1_w3, ln1_b3, ln2_w3, ln2_b3 = (a[:, None, :] for a in (ln1_w, ln1_b, ln2_w, ln2_b))

    xs = {CTX: x_prompt.reshape(N_CTX, D_MODEL), LAT: x_sample.reshape(N_LAT, D_MODEL)}
    new_k, new_v = [], []
    states = (jnp.zeros((BATCH, DEPTH, GLA_HEADS, GLA_DK, GLA_DV), F32),) * 2
    for l in range(DEPTH):
        for path in (CTX, LAT):
            x = xs[path]
            proj, proj_lr = _in_proj(path, x, mods, w_main, w_lr, l)
            if path.is_ctx:
                o_att = _attn_ctx(proj, attn_sink, l)
                o_gla, sf, sb = _gla(path, proj, proj_lr, wgf, wgb, bgf, bgb, norm_w, l, states=states)
                states = (sf, sb)
                new_k.append(proj[:, COL_K:COL_K + KV_WIDTH].reshape(BATCH, SEQ, ATTN_KV_HEADS, HEAD_DIM))
                new_v.append(proj[:, COL_V:COL_V + KV_WIDTH].reshape(BATCH, SEQ, ATTN_KV_HEADS, HEAD_DIM))
            else:
                o_att = _attn_lat(proj, attn_sink, cache_k, cache_v, cos_t, sin_t, l)
                (o_gla,) = _gla(path, proj, proj_lr, wgf, wgb, bgf, bgb, norm_w, l,
                                s0=(state_gla_fwd, state_gla_bwd))
            x = _out_proj(path, o_att, o_gla, x, mods, w_out_b, ln1_w3, ln1_b3, l)
            xs[path] = _ffn(path, x, mods, w_up_b, conv_p, w_down_b, ln2_w3, ln2_b3, l)

    return (xs[CTX].reshape(BATCH, SEQ, D_MODEL), xs[LAT].reshape(DEC_BATCH, DEC_SEQ, D_MODEL),
            jnp.stack(new_k, axis=1), jnp.stack(new_v, axis=1), states[0], states[1])
```

```python
import functools

import jax
import jax.numpy as jnp
from jax import lax
from jax.experimental import pallas as pl
from jax.experimental.pallas import tpu as pltpu

F32 = jnp.float32
BF16 = jnp.bfloat16

D_MODEL = 2048
BATCH = 32
SEQ = 256
DEPTH = 2
DEC_BATCH = 4
DEC_SEQ = 2048
PAST_LEN = 256
GRID_W = 64
HEAD_DIM = 128
ATTN_WIDTH = 1024
ATTN_HEADS = 8
ATTN_KV_HEADS = 2
ATTN_GROUP = 4
KV_WIDTH = 256
WINDOW = 128
BLOCK = 128
LAT_QBLOCKS = 2
GLA_WIDTH = 1024
GLA_HEADS = 4
GLA_DV = 256
GLA_DK = 128
GLA_KEY_WIDTH = 512
GATE_RANK = 16
GATE_NORM = 16.0
GLA_CHUNK = 64
GLA_GROUP = 256
GLA_CHAINS = 8
D_FF = 5632
ROPE_THETA = 10000.0
LN_EPS = 1e-5
NEG_INF = -1e30
DEEPNORM_ALPHA = (2 * DEPTH) ** 0.25

N_CTX = BATCH * SEQ
N_LAT = DEC_BATCH * DEC_SEQ
MOD_ROWS = 8
N_MODS = 6

PROJ_MAIN = 4608
COL_K = ATTN_WIDTH
COL_V = COL_K + KV_WIDTH
COL_GQ = COL_V + KV_WIDTH
COL_GK = COL_GQ + GLA_KEY_WIDTH
COL_GV = COL_GK + GLA_KEY_WIDTH
COL_GOG = COL_GV + GLA_WIDTH
LR_PAD = 128

VMEM_LIMIT = 56 * 1024 * 1024

TM_PROJ = 1024
TN_PROJ = 1536
TM_OUT = 512
OUT_ROW_BLOCK = 256
TM_FFN = 1024
TF_FFN = 512
FFN_M_SPLIT = 2
HALO = 16
GAP = 8
TN_ADA = 1024


class Path:
    def __init__(self, name, n_batch, seq):
        self.name = name
        self.n_batch = n_batch
        self.seq = seq
        self.rows = n_batch * seq
        self.is_ctx = name == "ctx"

    def mod_row(self, i, tm):
        return 0 if self.is_ctx else 1 + i // (self.seq // tm)


CTX = Path("ctx", BATCH, SEQ)
LAT = Path("lat", DEC_BATCH, DEC_SEQ)


def _params(sem):
    return pltpu.CompilerParams(dimension_semantics=sem, vmem_limit_bytes=VMEM_LIMIT)


def _silu(x):
    return x / (1.0 + jnp.exp(-x))


def _layer_norm(y, w, b):
    mu = jnp.mean(y, axis=-1, keepdims=True)
    d = y - mu
    var = jnp.mean(d * d, axis=-1, keepdims=True)
    return d * lax.rsqrt(var + LN_EPS) * w + b


def _dot(a, b):
    return jnp.dot(a, b, preferred_element_type=F32)


def _dot_nt(a, b):
    return lax.dot_general(a, b, (((1,), (1,)), ((), ())), preferred_element_type=F32)


def _dot_tn(a, b):
    return lax.dot_general(a, b, (((0,), (0,)), ((), ())), preferred_element_type=F32)


def _mod_spec(path, layer, chunk, tm):
    def index_map(i, *_):
        return ((layer * MOD_ROWS + path.mod_row(i, tm)) * N_MODS + chunk, 0, 0)
    return pl.BlockSpec((1, 1, D_MODEL), index_map)


def _layer_vec_spec(layer, width, n_grid):
    if n_grid == 1:
        return pl.BlockSpec((1, 1, width), lambda i: (layer, 0, 0))
    return pl.BlockSpec((1, 1, width), lambda i, j: (layer, 0, 0))


def _ada_kernel(c_ref, w_ref, b_ref, o_ref):
    s = _silu(c_ref[...]).astype(BF16)
    o_ref[0] = _dot(s, w_ref[0].astype(BF16)) + b_ref[0]


def _ada(cc, w_ada, b_ada):
    n_out = N_MODS * D_MODEL
    return pl.pallas_call(
        _ada_kernel,
        out_shape=jax.ShapeDtypeStruct((DEPTH, MOD_ROWS, n_out), F32),
        grid=(DEPTH, n_out // TN_ADA),
        in_specs=[
            pl.BlockSpec((MOD_ROWS, D_MODEL), lambda l, j: (0, 0)),
            pl.BlockSpec((1, D_MODEL, TN_ADA), lambda l, j: (l, 0, j)),
            pl.BlockSpec((1, 1, TN_ADA), lambda l, j: (l, 0, j)),
        ],
        out_specs=pl.BlockSpec((1, MOD_ROWS, TN_ADA), lambda l, j: (l, 0, j)),
        compiler_params=_params(("arbitrary", "arbitrary")),
        name="ada",
    )(cc, w_ada, b_ada.reshape(DEPTH, 1, n_out))


def _inproj_kernel(x_ref, sh_ref, sc_ref, w_ref, wlr_ref, o_ref, olr_ref, h_scr):
    @pl.when(pl.program_id(1) == 0)
    def _():
        h = (x_ref[...] * (1.0 + sc_ref[0]) + sh_ref[0]).astype(BF16)
        h_scr[...] = h
        olr_ref[...] = _dot(h, wlr_ref[0])
    o_ref[...] = _dot(h_scr[...], w_ref[0])


def _in_proj(path, x, mods, w_main, w_lr, layer):
    tm, tn = TM_PROJ, TN_PROJ
    return pl.pallas_call(
        _inproj_kernel,
        out_shape=(jax.ShapeDtypeStruct((path.rows, PROJ_MAIN), F32),
                   jax.ShapeDtypeStruct((path.rows, LR_PAD), F32)),
        grid=(path.rows // tm, PROJ_MAIN // tn),
        in_specs=[
            pl.BlockSpec((tm, D_MODEL), lambda i, j: (i, 0)),
            _mod_spec(path, layer, 0, tm),
            _mod_spec(path, layer, 1, tm),
            pl.BlockSpec((1, D_MODEL, tn), lambda i, j: (layer, 0, j)),
            pl.BlockSpec((1, D_MODEL, LR_PAD), lambda i, j: (layer, 0, 0)),
        ],
        out_specs=(pl.BlockSpec((tm, tn), lambda i, j: (i, j)),
                   pl.BlockSpec((tm, LR_PAD), lambda i, j: (i, 0))),
        scratch_shapes=[pltpu.VMEM((tm, D_MODEL), BF16)],
        compiler_params=_params(("arbitrary", "arbitrary")),
        name="in_proj_" + path.name,
    )(x, mods, mods, w_main, w_lr)


def _sink_softmax_pv(sinks, scores, values):
    heads = range(len(sinks))
    ms = [functools.reduce(jnp.maximum, [jnp.max(s, axis=-1, keepdims=True) for s in scores[h]],
                           sinks[h]) for h in heads]
    es = [[jnp.exp(s - ms[h]) for s in scores[h]] for h in heads]
    dens = [functools.reduce(lambda a, e: a + jnp.sum(e, axis=-1, keepdims=True), es[h],
                             jnp.exp(sinks[h] - ms[h])) for h in heads]
    invs = [1.0 / d for d in dens]
    ps = [[(e * invs[h]).astype(BF16) for e in es[h]] for h in heads]
    return [functools.reduce(lambda a, b: a + b, [_dot(p, v) for p, v in zip(ps[h], values[h])])
            for h in heads]


def _attn_ctx_kernel(sink_ref, q_ref, k_ref, v_ref, o_ref, *, layer):
    scale = HEAD_DIM ** -0.5
    heads = range(ATTN_HEADS)
    ks = [k_ref[:, kh * HEAD_DIM:(kh + 1) * HEAD_DIM].astype(BF16) for kh in range(ATTN_KV_HEADS)]
    vs = [v_ref[:, kh * HEAD_DIM:(kh + 1) * HEAD_DIM].astype(BF16) for kh in range(ATTN_KV_HEADS)]
    scores = [[_dot_nt(q_ref[:, h * HEAD_DIM:(h + 1) * HEAD_DIM].astype(BF16),
                       ks[h // ATTN_GROUP]) * scale] for h in heads]
    outs = _sink_softmax_pv([sink_ref[layer, h] for h in heads], scores,
                            [[vs[h // ATTN_GROUP]] for h in heads])
    for h in heads:
        o_ref[:, h * HEAD_DIM:(h + 1) * HEAD_DIM] = outs[h].astype(BF16)


def _attn_ctx(proj, sink, layer):
    return pl.pallas_call(
        functools.partial(_attn_ctx_kernel, layer=layer),
        out_shape=jax.ShapeDtypeStruct((N_CTX, ATTN_WIDTH), BF16),
        grid=(BATCH,),
        in_specs=[
            pl.BlockSpec(memory_space=pltpu.SMEM),
            pl.BlockSpec((SEQ, ATTN_WIDTH), lambda b: (b, 0)),
            pl.BlockSpec((SEQ, KV_WIDTH), lambda b: (b, COL_K // KV_WIDTH)),
            pl.BlockSpec((SEQ, KV_WIDTH), lambda b: (b, COL_V // KV_WIDTH)),
        ],
        out_specs=pl.BlockSpec((SEQ, ATTN_WIDTH), lambda b: (b, 0)),
        compiler_params=_params(("arbitrary",)),
        name="attn_ctx",
    )(sink, proj, proj, proj)


def _attn_lat_kernel(sink_ref, q_ref, k_ref, v_ref, ck_ref, cv_ref, cos_ref, sin_ref, o_ref, *,
                     layer):
    scale = HEAD_DIM ** -0.5
    win = 3 * BLOCK
    lane = lax.broadcasted_iota(jnp.int32, (1, HEAD_DIM), 1)
    low = (lane % (HEAD_DIM // 2)) < (HEAD_DIM // 4)

    def rope(x, c, s):
        partner = jnp.where(low, pltpu.roll(x, HEAD_DIM - HEAD_DIM // 4, 1),
                            pltpu.roll(x, HEAD_DIM // 4, 1))
        return x * c + partner * s

    kv_cols = [slice(kh * HEAD_DIM, (kh + 1) * HEAD_DIM) for kh in range(ATTN_KV_HEADS)]
    cks = [ck_ref[0, 0, :, c].astype(BF16) for c in kv_cols]
    cvs = [cv_ref[0, 0, :, c].astype(BF16) for c in kv_cols]
    sinks, scores, values, dests = [], [], [], []
    for sb in range(LAT_QBLOCKS):
        blk = pl.program_id(1) * LAT_QBLOCKS + sb
        q0 = pl.multiple_of(blk * BLOCK, BLOCK)
        start = pl.multiple_of(jnp.clip((blk - 1) * BLOCK, 0, DEC_SEQ - win), BLOCK)
        cos_q, sin_q = cos_ref[pl.ds(q0, BLOCK), :], sin_ref[pl.ds(q0, BLOCK), :]
        cos_k, sin_k = cos_ref[pl.ds(start, win), :], sin_ref[pl.ds(start, win), :]
        qpos = q0 + lax.broadcasted_iota(jnp.int32, (BLOCK, 1), 0)
        kpos = start + lax.broadcasted_iota(jnp.int32, (1, win), 1)
        valid = jnp.abs(qpos - kpos) <= WINDOW
        kws = [rope(k_ref[pl.ds(start, win), c], cos_k, sin_k).astype(BF16) for c in kv_cols]
        vws = [v_ref[pl.ds(start, win), c].astype(BF16) for c in kv_cols]
        rows = slice(sb * BLOCK, (sb + 1) * BLOCK)
        for h in range(ATTN_HEADS):
            kh = h // ATTN_GROUP
            q = rope(q_ref[rows, h * HEAD_DIM:(h + 1) * HEAD_DIM], cos_q, sin_q).astype(BF16)
            scores.append([jnp.where(valid, _dot_nt(q, kws[kh]) * scale, NEG_INF),
                           _dot_nt(q, cks[kh]) * scale])
            values.append([vws[kh], cvs[kh]])
            sinks.append(sink_ref[layer, h])
            dests.append((rows, h))
    outs = _sink_softmax_pv(sinks, scores, values)
    for (rows, h), o in zip(dests, outs):
        o_ref[rows, h * HEAD_DIM:(h + 1) * HEAD_DIM] = o.astype(BF16)


def _attn_lat(proj, sink, cache_k, cache_v, cos_t, sin_t, layer):
    nb = DEC_SEQ // (BLOCK * LAT_QBLOCKS)
    qrows = BLOCK * LAT_QBLOCKS
    cache_spec = pl.BlockSpec((1, 1, PAST_LEN, KV_WIDTH), lambda b, i: (b, layer, 0, 0))
    table_spec = pl.BlockSpec((DEC_SEQ, HEAD_DIM), lambda b, i: (0, 0))
    return pl.pallas_call(
        functools.partial(_attn_lat_kernel, layer=layer),
        out_shape=jax.ShapeDtypeStruct((N_LAT, ATTN_WIDTH), BF16),
        grid=(DEC_BATCH, nb),
        in_specs=[
            pl.BlockSpec(memory_space=pltpu.SMEM),
            pl.BlockSpec((qrows, ATTN_WIDTH), lambda b, i: (b * nb + i, 0)),
            pl.BlockSpec((DEC_SEQ, KV_WIDTH), lambda b, i: (b, COL_K // KV_WIDTH)),
            pl.BlockSpec((DEC_SEQ, KV_WIDTH), lambda b, i: (b, COL_V // KV_WIDTH)),
            cache_spec, cache_spec, table_spec, table_spec,
        ],
        out_specs=pl.BlockSpec((qrows, ATTN_WIDTH), lambda b, i: (b * nb + i, 0)),
        compiler_params=_params(("arbitrary", "arbitrary")),
        name="attn_lat",
    )(sink, proj, proj, proj,
      cache_k.reshape(DEC_BATCH, DEPTH, PAST_LEN, KV_WIDTH),
      cache_v.reshape(DEC_BATCH, DEPTH, PAST_LEN, KV_WIDTH), cos_t, sin_t)


def _rope_tables():
    half = HEAD_DIM // 2
    n_freq = half // 2
    t = jnp.arange(DEC_SEQ)
    row = (t // GRID_W).astype(F32)
    col = (t % GRID_W).astype(F32)
    freqs = ROPE_THETA ** (-jnp.arange(n_freq, dtype=F32) / n_freq)
    tabs = []
    for pos in (row, col):
        ang = pos[:, None] * freqs[None, :]
        tabs.append((jnp.cos(ang), jnp.sin(ang)))
    cos_t = jnp.concatenate([tabs[0][0], tabs[0][0], tabs[1][0], tabs[1][0]], axis=-1)
    sin_t = jnp.concatenate([-tabs[0][1], tabs[0][1], -tabs[1][1], tabs[1][1]], axis=-1)
    return cos_t, sin_t


def _log_sigmoid(x):
    return jnp.minimum(x, 0.0) - jnp.log1p(jnp.exp(-jnp.abs(x)))


def _gla_kernel(*refs, seq, n_heads, v_split, is_ctx, gpi):
    q_ref, k_ref = refs[:2]
    v_refs = refs[2:2 + v_split]
    og_refs = refs[2 + v_split:2 + 2 * v_split]
    n_in = 8 + 2 * v_split
    lr_ref, wgf_ref, wgb_ref, bgf_ref, bgb_ref, nw_ref = refs[2 + 2 * v_split:n_in]
    if is_ctx:
        o_ref, sf_ref, sb_ref = refs[-16:-13]
    else:
        s0f_ref, s0b_ref = refs[n_in:n_in + 2]
        o_ref = refs[-14]
    heads_per_block = n_heads // v_split

    def head_cols(block_refs, h):
        hb = h % heads_per_block
        return block_refs[h // heads_per_block], slice(hb * GLA_DV, (hb + 1) * GLA_DV)
    (gf_scr, gb_scr, qef_scr, qeb_scr, o_scr, uf_scr, ub_scr, sinf_scr, sinb_scr,
     decf_scr, decb_scr, stf_scr, stb_scr) = refs[-13:]
    c = GLA_CHUNK
    grp = GLA_GROUP
    cpg = grp // c
    n_chunks = seq // c
    n_groups = seq // grp
    q_scale = GLA_DK ** -0.5

    lr = lr_ref[...].astype(BF16)
    gf_scr[...] = _log_sigmoid(_dot(lr, wgf_ref[0]) + bgf_ref[0]) / GATE_NORM
    gb_scr[...] = _log_sigmoid(_dot(lr, wgb_ref[0]) + bgb_ref[0]) / GATE_NORM

    ri = lax.broadcasted_iota(jnp.int32, (grp, grp), 0)
    ci = lax.broadcasted_iota(jnp.int32, (grp, grp), 1)
    same_chunk = (ri // c) == (ci // c)
    keep_f = same_chunk & (ri >= ci)
    keep_b = same_chunk & (ri <= ci)

    pos = lax.broadcasted_iota(jnp.int32, (grp, 1), 0) & (c - 1)

    def cumsum(forward, g):
        y = g
        s = 1
        while s < c:
            if forward:
                y = y + jnp.where(pos >= s, pltpu.roll(y, s, 0), 0.0)
            else:
                y = y + jnp.where(pos < c - s, pltpu.roll(y, grp - s, 0), 0.0)
            s *= 2
        return y

    dirs = ((gf_scr, keep_f, c - 1, qef_scr, uf_scr, decf_scr),
            (gb_scr, keep_b, 0, qeb_scr, ub_scr, decb_scr))

    hks = [slice(h * GLA_DK, (h + 1) * GLA_DK) for h in range(n_heads)]
    hvs = [slice(h * GLA_DV, (h + 1) * GLA_DV) for h in range(n_heads)]

    def group_body(it, carry):
        items = [(it * gpi + gi, h) for gi in range(gpi) for h in range(n_heads)]
        rows = [pl.ds(pl.multiple_of(r * grp, grp), grp) for r, _ in items]
        chains = [(x, d) for x in range(len(items)) for d in range(2)]
        qs = [q_ref[rows[x], hks[h]] * q_scale for x, (_, h) in enumerate(items)]
        ks = [k_ref[rows[x], hks[h]] for x, (_, h) in enumerate(items)]
        vs = [head_cols(v_refs, h)[0][rows[x], head_cols(v_refs, h)[1]].astype(BF16)
              for x, (_, h) in enumerate(items)]
        bs = [cumsum(d == 0, dirs[d][0][rows[x], hks[items[x][1]]]) for x, d in chains]
        ebs = [jnp.exp(b) for b in bs]
        embs = [jnp.exp(-b) for b in bs]
        qes = [(qs[x] * ebs[i]).astype(BF16) for i, (x, d) in enumerate(chains)]
        kes = [ks[x] * embs[i] for i, (x, d) in enumerate(chains)]
        for i, (x, d) in enumerate(chains):
            dirs[d][3][rows[x], hks[items[x][1]]] = qes[i]
        avals = [jnp.where(dirs[d][1], _dot_nt(qes[i], kes[i].astype(BF16)), 0.0)
                 for i, (x, d) in enumerate(chains)]
        decays = [[jnp.exp(bs[i][cc * c + dirs[d][2]:cc * c + dirs[d][2] + 1, :])
                   for cc in range(cpg)] for i, (x, d) in enumerate(chains)]
        kds = [[(kes[i][cc * c:(cc + 1) * c] * decays[i][cc]).astype(BF16) for cc in range(cpg)]
               for i in range(len(chains))]
        for i, (x, d) in enumerate(chains):
            r, h = items[x]
            for cc in range(cpg):
                n = r * cpg + cc
                dirs[d][4][h, n] = _dot_tn(vs[x][cc * c:(cc + 1) * c], kds[i][cc])
                dirs[d][5][h, n] = jnp.broadcast_to(decays[i][cc], (8, GLA_DK))
        for x, (_, h) in enumerate(items):
            a_sum = (avals[2 * x] + avals[2 * x + 1]).astype(BF16)
            o_scr[rows[x], hvs[h]] = _dot(a_sum, vs[x])
        return carry

    lax.fori_loop(0, n_groups // gpi, group_body, 0)

    for h in range(n_heads):
        if is_ctx:
            stf_scr[h] = jnp.zeros((GLA_DV, GLA_DK), F32)
            stb_scr[h] = jnp.zeros((GLA_DV, GLA_DK), F32)
        else:
            stf_scr[h] = s0f_ref[0, 0, h].T
            stb_scr[h] = s0b_ref[0, 0, h].T

    def scan_body(n, carry):
        nb = n_chunks - 1 - n
        for h in range(n_heads):
            s = stf_scr[h]
            sinf_scr[h, n] = s.astype(BF16)
            stf_scr[h] = decf_scr[h, n][0:1, :] * s + uf_scr[h, n]
            s = stb_scr[h]
            sinb_scr[h, nb] = s.astype(BF16)
            stb_scr[h] = decb_scr[h, nb][0:1, :] * s + ub_scr[h, nb]
        return carry

    lax.fori_loop(0, n_chunks, scan_body, 0)
    if is_ctx:
        for h in range(n_heads):
            sf_ref[0, 0, h] = stf_scr[h].T
            sb_ref[0, 0, h] = stb_scr[h].T

    nw = nw_ref[0]

    def finish(it, carry):
        items = [(it * gpi + gi, h) for gi in range(gpi) for h in range(n_heads)]
        rows = [pl.ds(pl.multiple_of(r * grp, grp), grp) for r, _ in items]
        inter = []
        for r, h in items:
            parts = []
            for cc in range(cpg):
                n = r * cpg + cc
                cr = pl.ds(pl.multiple_of(r * grp + cc * c, c), c)
                qe2 = jnp.concatenate([qef_scr[cr, hks[h]], qeb_scr[cr, hks[h]]], axis=1)
                s2 = jnp.concatenate([sinf_scr[h, n], sinb_scr[h, n]], axis=1)
                parts.append(_dot_nt(qe2, s2))
            inter.append(jnp.concatenate(parts, axis=0))
        os = [o_scr[rows[x], hvs[h]] + inter[x] for x, (_, h) in enumerate(items)]
        scales = [lax.rsqrt(jnp.mean(o * o, axis=-1, keepdims=True) + LN_EPS) for o in os]
        gates = [_silu(head_cols(og_refs, h)[0][rows[x], head_cols(og_refs, h)[1]])
                 for x, (_, h) in enumerate(items)]
        for x, (_, h) in enumerate(items):
            o_ref[rows[x], hvs[h]] = (os[x] * scales[x] * nw * gates[x]).astype(BF16)
        return carry

    lax.fori_loop(0, n_groups // gpi, finish, 0)


def _gla(path, proj, proj_lr, wgf, wgb, bgf, bgb, norm_w, layer, *, s0=None, states=None):
    seq = path.seq
    n_heads = GLA_HEADS if path.is_ctx else 1
    n_chunks = seq // GLA_CHUNK
    n_groups = seq // GLA_GROUP

    v_split = 2 if n_heads == GLA_HEADS else 1

    def col_spec(width, col, split=1, part=0):
        w = n_heads * width // split
        assert col % w == 0
        return pl.BlockSpec((seq, w), lambda b, h: (b, col // w + h * split + part))

    head_w = pl.BlockSpec((1, LR_PAD, n_heads * GLA_DK), lambda b, h: (layer, 0, h))
    head_b = pl.BlockSpec((1, 1, n_heads * GLA_DK), lambda b, h: (layer, 0, h))
    state_spec = pl.BlockSpec((1, 1, n_heads, GLA_DK, GLA_DV), lambda b, h: (b, layer, h, 0, 0))
    in_specs = [
        col_spec(GLA_DK, COL_GQ), col_spec(GLA_DK, COL_GK),
        *[col_spec(GLA_DV, COL_GV, v_split, p) for p in range(v_split)],
        *[col_spec(GLA_DV, COL_GOG, v_split, p) for p in range(v_split)],
        pl.BlockSpec((seq, LR_PAD), lambda b, h: (b, 0)),
        head_w, head_w, head_b, head_b,
        pl.BlockSpec((1, 1, GLA_DV), lambda b, h: (layer, 0, 0)),
    ]
    args = [proj] * (2 + 2 * v_split) + [proj_lr, wgf, wgb, bgf, bgb, norm_w]
    o_shape = jax.ShapeDtypeStruct((path.rows, GLA_WIDTH), BF16)
    o_spec = pl.BlockSpec((seq, n_heads * GLA_DV), lambda b, h: (b, h))
    aliases = {}
    if path.is_ctx:
        state_shape = jax.ShapeDtypeStruct((BATCH, DEPTH, GLA_HEADS, GLA_DK, GLA_DV), F32)
        out_shape = (o_shape, state_shape, state_shape)
        out_specs = (o_spec, state_spec, state_spec)
        in_specs += [pl.BlockSpec(memory_space=pl.ANY), pl.BlockSpec(memory_space=pl.ANY)]
        args += list(states)
        aliases = {len(args) - 2: 1, len(args) - 1: 2}
    else:
        in_specs += [state_spec, state_spec]
        args += list(s0)
        out_shape = (o_shape,)
        out_specs = (o_spec,)
    per_chunk = (n_heads, n_chunks, GLA_DV, GLA_DK)
    return pl.pallas_call(
        functools.partial(_gla_kernel, seq=seq, n_heads=n_heads, v_split=v_split, is_ctx=path.is_ctx,
                          gpi=min(n_groups, GLA_CHAINS // (2 * n_heads))),
        out_shape=out_shape,
        grid=(path.n_batch, GLA_HEADS // n_heads),
        in_specs=in_specs,
        out_specs=out_specs,
        input_output_aliases=aliases,
        scratch_shapes=[
            pltpu.VMEM((seq, n_heads * GLA_DK), F32), pltpu.VMEM((seq, n_heads * GLA_DK), F32),
            pltpu.VMEM((seq, n_heads * GLA_DK), BF16), pltpu.VMEM((seq, n_heads * GLA_DK), BF16),
            pltpu.VMEM((seq, n_heads * GLA_DV), F32),
            pltpu.VMEM(per_chunk, F32), pltpu.VMEM(per_chunk, F32),
            pltpu.VMEM(per_chunk, BF16), pltpu.VMEM(per_chunk, BF16),
            pltpu.VMEM((n_heads, n_chunks, 8, GLA_DK), F32),
            pltpu.VMEM((n_heads, n_chunks, 8, GLA_DK), F32),
            pltpu.VMEM((n_heads, GLA_DV, GLA_DK), F32), pltpu.VMEM((n_heads, GLA_DV, GLA_DK), F32),
        ],
        compiler_params=_params(("arbitrary", "arbitrary")),
        name="gla_" + path.name,
    )(*args)


def _outproj_kernel(oa_ref, og_ref, x_ref, g_ref, w_ref, lw_ref, lb_ref, o_ref):
    rb = OUT_ROW_BLOCK
    blocks = [slice(r0, r0 + rb) for r0 in range(0, o_ref.shape[0], rb)]

    def matmul(r):
        return (_dot(oa_ref[r, :], w_ref[0, :ATTN_WIDTH, :])
                + _dot(og_ref[r, :], w_ref[0, ATTN_WIDTH:, :]))

    def norm(r, f):
        y = DEEPNORM_ALPHA * x_ref[r, :] + g_ref[0] * f
        o_ref[r, :] = _layer_norm(y, lw_ref[0], lb_ref[0])

    f_prev = matmul(blocks[0])
    for k in range(1, len(blocks)):
        f_next = matmul(blocks[k])
        norm(blocks[k - 1], f_prev)
        f_prev = f_next
    norm(blocks[-1], f_prev)


def _out_proj(path, o_att, o_gla, x, mods, w_out, ln_w, ln_b, layer):
    tm = TM_OUT
    vec = _layer_vec_spec(layer, D_MODEL, 1)
    return pl.pallas_call(
        _outproj_kernel,
        out_shape=jax.ShapeDtypeStruct((path.rows, D_MODEL), F32),
        grid=(path.rows // tm,),
        in_specs=[
            pl.BlockSpec((tm, ATTN_WIDTH), lambda i: (i, 0)),
            pl.BlockSpec((tm, GLA_WIDTH), lambda i: (i, 0)),
            pl.BlockSpec((tm, D_MODEL), lambda i: (i, 0)),
            _mod_spec(path, layer, 2, tm),
            pl.BlockSpec((1, D_MODEL, D_MODEL), lambda i: (layer, 0, 0)),
            vec, vec,
        ],
        out_specs=pl.BlockSpec((tm, D_MODEL), lambda i: (i, 0)),
        compiler_params=_params(("arbitrary",)),
        name="out_proj_" + path.name,
    )(o_att, o_gla, x, mods, w_out, ln_w, ln_b)


def _ffn_kernel(*refs, tm, nf, seq, has_halo):
    if has_halo:
        x_ref, xp_ref, xn_ref = refs[:3]
        refs = refs[3:]
    else:
        x_ref = refs[0]
        refs = refs[1:]
    (sh_ref, sc_ref, g_ref, wa_ref, wg_ref, cp_ref, wd_ref, lw_ref,
     lb_ref, o_ref, h_scr, ua0_scr, ug0_scr, ua1_scr, ug1_scr, act0_scr, act1_scr) = refs
    i = pl.program_id(0)
    j = pl.program_id(1)
    u_slots = ((ua0_scr, ug0_scr), (ua1_scr, ug1_scr))
    act_slots = (act0_scr, act1_scr)
    tf = wa_ref.shape[-1]
    hrows = HALO if has_halo else 0
    h_rows = tm + 2 * hrows
    seg = tm if has_halo else seq

    def u_row(r):
        return r if has_halo else GAP + r + (r // seg) * GAP

    up_n, up_m = tf // 2, h_rows // FFN_M_SPLIT
    act_r, act_c = 128, 128
    down_m, down_n = tm // FFN_M_SPLIT, D_MODEL // 4

    def up_piece(slot, which, n0, m0):
        w_ref = (wa_ref, wg_ref)[which]
        u = _dot(h_scr[m0:m0 + up_m, :], w_ref[0, :, n0:n0 + up_n])
        run_len = up_m if has_halo else seg
        for r in range(0, up_m, run_len):
            s0 = u_row(m0 + r)
            u_slots[slot][which][s0:s0 + run_len, n0:n0 + up_n] = u[r:r + run_len]

    def down_piece(slot, m0, n0):
        o_ref[m0:m0 + down_m, n0:n0 + down_n] += _dot(act_slots[slot][m0:m0 + down_m, :],
                                                      wd_ref[0, :, n0:n0 + down_n])

    def act_piece(slot, r0, c0):
        cols = slice(c0, c0 + act_c)
        s0 = u_row(r0 + hrows)

        def conv(u_ref, which):
            cp = cp_ref[0, 0, :, which * tf + c0:which * tf + c0 + act_c]
            return (u_ref[s0 - 1:s0 - 1 + act_r, cols] * cp[0:1] + u_ref[s0:s0 + act_r, cols] * cp[1:2]
                    + u_ref[s0 + 1:s0 + 1 + act_r, cols] * cp[2:3] + cp[3:4])

        a = conv(u_slots[slot][0], 0)
        g = conv(u_slots[slot][1], 1)
        act_slots[slot][r0:r0 + act_r, cols] = (_silu(g) * a).astype(BF16)

    def run(up=None, act=None, down=None):
        pieces = []

        def add(fn, arg_list):
            for k, a in enumerate(arg_list):
                pieces.append(((k + 0.5) / len(arg_list), len(pieces), fn, a))

        if up is not None:
            add(up_piece, [(up, w, n0, m0) for w in (0, 1) for n0 in range(0, tf, up_n)
                           for m0 in range(0, h_rows, up_m)])
        if act is not None:
            add(act_piece, [(act, r0, c0) for c0 in range(0, tf, act_c)
                            for r0 in range(0, tm, act_r)])
        if down is not None:
            add(down_piece, [(down, m0, n0) for n0 in range(0, D_MODEL, down_n)
                             for m0 in range(0, tm, down_m)])
        for _, _, fn, a in sorted(pieces, key=lambda t: t[:2]):
            fn(*a)

    @pl.when(j == 0)
    def _():
        sc = 1.0 + sc_ref[0]
        sh = sh_ref[0]

        def build_h(part):
            xr = slice(part * (tm // 2), (part + 1) * (tm // 2))
            h_scr[hrows + xr.start:hrows + xr.stop, :] = (x_ref[xr, :] * sc + sh).astype(BF16)
            if has_halo:
                tile_in_seq = i % (seq // tm)
                if part == 0:
                    h_scr[0:HALO, :] = jnp.where(tile_in_seq != 0, xp_ref[...] * sc + sh,
                                                 0.0).astype(BF16)
                else:
                    h_scr[HALO + tm:2 * HALO + tm, :] = jnp.where(
                        tile_in_seq != seq // tm - 1, xn_ref[...] * sc + sh, 0.0).astype(BF16)

        if not has_halo:
            for pair in u_slots:
                for u_scr in pair:
                    for r in range(0, tm + 1, seg):
                        u_scr[u_row(r) - GAP:u_row(r), :] = jnp.zeros((GAP, tf), F32)
        build_h(0)
        if FFN_M_SPLIT == 1:
            build_h(1)
        pieces = [(0, w, n0, m0) for m0 in range(0, h_rows, up_m) for w in (0, 1)
                  for n0 in range(0, tf, up_n)]
        zr = tm // len(pieces)
        for k, a in enumerate(pieces):
            up_piece(*a)
            if k == 0 and FFN_M_SPLIT == 2:
                build_h(1)
            o_ref[k * zr:(k + 1) * zr, :] = jnp.zeros((zr, D_MODEL), F32)

    @pl.when(j == 1)
    def _():
        run(up=1, act=0)

    for parity in (0, 1):
        @pl.when((j >= 2) & (j < nf) & (j % 2 == parity))
        def _(parity=parity):
            run(up=parity, act=1 - parity, down=parity)

    @pl.when(j == nf)
    def _():
        run(act=(nf - 1) % 2, down=nf % 2)

    @pl.when(j == nf + 1)
    def _():
        for m0 in range(0, tm, down_m):
            for n0 in range(0, D_MODEL, down_n):
                down_piece((nf - 1) % 2, m0, n0)
        for m0 in range(0, tm, down_m):
            rows = slice(m0, m0 + down_m)
            y = DEEPNORM_ALPHA * x_ref[rows, :] + g_ref[0] * o_ref[rows, :]
            o_ref[rows, :] = _layer_norm(y, lw_ref[0], lb_ref[0])


def _pack_conv_params(conv_w, conv_b):
    nf = D_FF // TF_FFN
    cw = conv_w.reshape(DEPTH, 3, 2, nf, TF_FFN).transpose(0, 3, 1, 2, 4).reshape(DEPTH, nf, 3, 2 * TF_FFN)
    cb = conv_b.reshape(DEPTH, 1, 2, nf, TF_FFN).transpose(0, 3, 1, 2, 4).reshape(DEPTH, nf, 1, 2 * TF_FFN)
    return jnp.concatenate([cw, cb, jnp.zeros((DEPTH, nf, 4, 2 * TF_FFN), F32)], axis=2)


def _ffn(path, x, mods, w_up, conv_p, w_down, ln_w, ln_b, layer):
    tm, tf = TM_FFN, TF_FFN
    nf = D_FF // tf
    hb = tm // HALO
    n_halo_blocks = path.rows // HALO
    has_halo = path.seq > tm
    vec = _layer_vec_spec(layer, D_MODEL, 2)

    def chunk(j, lag):
        return jnp.clip(j - lag, 0, nf - 1)

    x_specs = [pl.BlockSpec((tm, D_MODEL), lambda i, j: (i, 0))]
    x_args = [x]
    if has_halo:
        x_specs += [
            pl.BlockSpec((HALO, D_MODEL), lambda i, j: (jnp.maximum(i * hb - 1, 0), 0)),
            pl.BlockSpec((HALO, D_MODEL),
                         lambda i, j: (jnp.minimum((i + 1) * hb, n_halo_blocks - 1), 0)),
        ]
        x_args += [x, x]
    h_rows = tm + 2 * HALO if has_halo else tm
    u_rows = tm + 2 * HALO if has_halo else GAP + (tm // path.seq) * (path.seq + GAP)
    u_scratch = pltpu.VMEM((u_rows, tf), F32)
    act_scratch = pltpu.VMEM((tm, tf), BF16)
    return pl.pallas_call(
        functools.partial(_ffn_kernel, tm=tm, nf=nf, seq=path.seq, has_halo=has_halo),
        out_shape=jax.ShapeDtypeStruct((path.rows, D_MODEL), F32),
        grid=(path.rows // tm, nf + 2),
        in_specs=x_specs + [
            _mod_spec(path, layer, 3, tm), _mod_spec(path, layer, 4, tm),
            _mod_spec(path, layer, 5, tm),
            pl.BlockSpec((1, D_MODEL, tf), lambda i, j: (layer, 0, chunk(j, 0))),
            pl.BlockSpec((1, D_MODEL, tf), lambda i, j: (layer, 0, nf + chunk(j, 0))),
            pl.BlockSpec((1, 1, 8, 2 * tf), lambda i, j: (layer, chunk(j, 1), 0, 0)),
            pl.BlockSpec((1, tf, D_MODEL), lambda i, j: (layer, chunk(j, 2), 0)),
            vec, vec,
        ],
        out_specs=pl.BlockSpec((tm, D_MODEL), lambda i, j: (i, 0), pipeline_mode=pl.Buffered(1)),
        scratch_shapes=[pltpu.VMEM((h_rows, D_MODEL), BF16),
                        u_scratch, u_scratch, u_scratch, u_scratch, act_scratch, act_scratch],
        compiler_params=_params(("arbitrary", "arbitrary")),
        name="ffn_" + path.name,
    )(*x_args, mods, mods, mods, w_up, w_up, conv_p, w_down, ln_w, ln_b)


def _pad_gate(w, row0):
    return jnp.pad(w, ((0, 0), (row0, LR_PAD - GATE_RANK - row0), (0, 0))).astype(BF16)


def kernel(x_prompt, x_sample, cache_k, cache_v, state_gla_fwd, state_gla_bwd, c, c_ctx, w_ada,
           b_ada, w_in, attn_sink, w_gate_f, b_gate_f, w_gate_b, b_gate_b, gla_norm_w, w_out,
           ln1_w, ln1_b, w_up, conv_w, conv_b, w_down, ln2_w, ln2_b):
    cc = jnp.concatenate([c_ctx[None, :], c, jnp.zeros((MOD_ROWS - 1 - DEC_BATCH, D_MODEL), F32)], axis=0)
    mods = _ada(cc, w_ada, b_ada).reshape(DEPTH * MOD_ROWS * N_MODS, 1, D_MODEL)
    cos_t, sin_t = _rope_tables()

    w_main = w_in[:, :, :PROJ_MAIN].astype(BF16)
    w_lr = jnp.pad(w_in[:, :, PROJ_MAIN:], ((0, 0), (0, 0), (0, LR_PAD - 2 * GATE_RANK))).astype(BF16)
    w_up_b = w_up.astype(BF16)
    w_down_b = w_down.astype(BF16)
    w_out_b = w_out.astype(BF16)
    wgf = _pad_gate(w_gate_f, 0)
    wgb = _pad_gate(w_gate_b, GATE_RANK)
    bgf = b_gate_f[:, None, :]
    bgb = b_gate_b[:, None, :]
    norm_w = gla_norm_w[:, None, :]
    conv_p = _pack_conv_params(conv_w, conv_b)
    ln1_w3, ln1_b3, ln2_w3, ln2_b3 = (a[:, None, :] for a in (ln1_w, ln1_b, ln2_w, ln2_b))

    xs = {CTX: x_prompt.reshape(N_CTX, D_MODEL), LAT: x_sample.reshape(N_LAT, D_MODEL)}
    new_k, new_v = [], []
    states = (jnp.zeros((BATCH, DEPTH, GLA_HEADS, GLA_DK, GLA_DV), F32),) * 2
    for l in range(DEPTH):
        for path in (CTX, LAT):
            x = xs[path]
            proj, proj_lr = _in_proj(path, x, mods, w_main, w_lr, l)
            if path.is_ctx:
                o_att = _attn_ctx(proj, attn_sink, l)
                o_gla, sf, sb = _gla(path, proj, proj_lr, wgf, wgb, bgf, bgb, norm_w, l, states=states)
                states = (sf, sb)
                new_k.append(proj[:, COL_K:COL_K + KV_WIDTH].reshape(BATCH, SEQ, ATTN_KV_HEADS, HEAD_DIM))
                new_v.append(proj[:, COL_V:COL_V + KV_WIDTH].reshape(BATCH, SEQ, ATTN_KV_HEADS, HEAD_DIM))
            else:
                o_att = _attn_lat(proj, attn_sink, cache_k, cache_v, cos_t, sin_t, l)
                (o_gla,) = _gla(path, proj, proj_lr, wgf, wgb, bgf, bgb, norm_w, l,
                                s0=(state_gla_fwd, state_gla_bwd))
            x = _out_proj(path, o_att, o_gla, x, mods, w_out_b, ln1_w3, ln1_b3, l)
            xs[path] = _ffn(path, x, mods, w_up_b, conv_p, w_down_b, ln2_w3, ln2_b3, l)

    return (xs[CTX].reshape(BATCH, SEQ, D_MODEL), xs[LAT].reshape(DEC_BATCH, DEC_SEQ, D_MODEL),
            jnp.stack(new_k, axis=1), jnp.stack(new_v, axis=1), states[0], states[1])
```

```python
import functools

import jax
import jax.numpy as jnp
from jax import lax
from jax.experimental import pallas as pl
from jax.experimental.pallas import tpu as pltpu

F32 = jnp.float32
BF16 = jnp.bfloat16

D_MODEL = 2048
BATCH = 32
SEQ = 256
DEPTH = 2
DEC_BATCH = 4
DEC_SEQ = 2048
PAST_LEN = 256
GRID_W = 64
HEAD_DIM = 128
ATTN_WIDTH = 1024
ATTN_HEADS = 8
ATTN_KV_HEADS = 2
ATTN_GROUP = 4
KV_WIDTH = 256
WINDOW = 128
BLOCK = 128
LAT_QBLOCKS = 2
GLA_WIDTH = 1024
GLA_HEADS = 4
GLA_DV = 256
GLA_DK = 128
GLA_KEY_WIDTH = 512
GATE_RANK = 16
GATE_NORM = 16.0
GLA_CHUNK = 64
GLA_GROUP = 256
GLA_CHAINS = 8
D_FF = 5632
ROPE_THETA = 10000.0
LN_EPS = 1e-5
NEG_INF = -1e30
DEEPNORM_ALPHA = (2 * DEPTH) ** 0.25

N_CTX = BATCH * SEQ
N_LAT = DEC_BATCH * DEC_SEQ
MOD_ROWS = 8
N_MODS = 6

PROJ_MAIN = 4608
COL_K = ATTN_WIDTH
COL_V = COL_K + KV_WIDTH
COL_GQ = COL_V + KV_WIDTH
COL_GK = COL_GQ + GLA_KEY_WIDTH
COL_GV = COL_GK + GLA_KEY_WIDTH
COL_GOG = COL_GV + GLA_WIDTH
LR_PAD = 128

VMEM_LIMIT = 56 * 1024 * 1024

TM_PROJ = 1024
TN_PROJ = 1536
TM_OUT = 512
OUT_ROW_BLOCK = 256
TM_FFN = 1024
TF_FFN = 512
FFN_M_SPLIT = 2
HALO = 16
GAP = 8
TN_ADA = 1024


class Path:
    def __init__(self, name, n_batch, seq):
        self.name = name
        self.n_batch = n_batch
        self.seq = seq
        self.rows = n_batch * seq
        self.is_ctx = name == "ctx"

    def mod_row(self, i, tm):
        return 0 if self.is_ctx else 1 + i // (self.seq // tm)


CTX = Path("ctx", BATCH, SEQ)
LAT = Path("lat", DEC_BATCH, DEC_SEQ)


def _params(sem):
    return pltpu.CompilerParams(dimension_semantics=sem, vmem_limit_bytes=VMEM_LIMIT)


def _silu(x):
    return x / (1.0 + jnp.exp(-x))


def _layer_norm(y, w, b):
    mu = jnp.mean(y, axis=-1, keepdims=True)
    d = y - mu
    var = jnp.mean(d * d, axis=-1, keepdims=True)
    return d * lax.rsqrt(var + LN_EPS) * w + b


def _dot(a, b):
    return jnp.dot(a, b, preferred_element_type=F32)


def _dot_nt(a, b):
    return lax.dot_general(a, b, (((1,), (1,)), ((), ())), preferred_element_type=F32)


def _dot_tn(a, b):
    return lax.dot_general(a, b, (((0,), (0,)), ((), ())), preferred_element_type=F32)


def _mod_spec(path, layer, chunk, tm):
    def index_map(i, *_):
        return ((layer * MOD_ROWS + path.mod_row(i, tm)) * N_MODS + chunk, 0, 0)
    return pl.BlockSpec((1, 1, D_MODEL), index_map)


def _layer_vec_spec(layer, width, n_grid):
    if n_grid == 1:
        return pl.BlockSpec((1, 1, width), lambda i: (layer, 0, 0))
    return pl.BlockSpec((1, 1, width), lambda i, j: (layer, 0, 0))


def _ada_kernel(c_ref, w_ref, b_ref, o_ref):
    s = _silu(c_ref[...]).astype(BF16)
    o_ref[0] = _dot(s, w_ref[0].astype(BF16)) + b_ref[0]


def _ada(cc, w_ada, b_ada):
    n_out = N_MODS * D_MODEL
    return pl.pallas_call(
        _ada_kernel,
        out_shape=jax.ShapeDtypeStruct((DEPTH, MOD_ROWS, n_out), F32),
        grid=(DEPTH, n_out // TN_ADA),
        in_specs=[
            pl.BlockSpec((MOD_ROWS, D_MODEL), lambda l, j: (0, 0)),
            pl.BlockSpec((1, D_MODEL, TN_ADA), lambda l, j: (l, 0, j)),
            pl.BlockSpec((1, 1, TN_ADA), lambda l, j: (l, 0, j)),
        ],
        out_specs=pl.BlockSpec((1, MOD_ROWS, TN_ADA), lambda l, j: (l, 0, j)),
        compiler_params=_params(("arbitrary", "arbitrary")),
        name="ada",
    )(cc, w_ada, b_ada.reshape(DEPTH, 1, n_out))


def _inproj_kernel(x_ref, sh_ref, sc_ref, w_ref, wlr_ref, o_ref, olr_ref, h_scr):
    @pl.when(pl.program_id(1) == 0)
    def _():
        h = (x_ref[...] * (1.0 + sc_ref[0]) + sh_ref[0]).astype(BF16)
        h_scr[...] = h
        olr_ref[...] = _dot(h, wlr_ref[0])
    o_ref[...] = _dot(h_scr[...], w_ref[0])


def _in_proj(path, x, mods, w_main, w_lr, layer):
    tm, tn = TM_PROJ, TN_PROJ
    return pl.pallas_call(
        _inproj_kernel,
        out_shape=(jax.ShapeDtypeStruct((path.rows, PROJ_MAIN), F32),
                   jax.ShapeDtypeStruct((path.rows, LR_PAD), F32)),
        grid=(path.rows // tm, PROJ_MAIN // tn),
        in_specs=[
            pl.BlockSpec((tm, D_MODEL), lambda i, j: (i, 0)),
            _mod_spec(path, layer, 0, tm),
            _mod_spec(path, layer, 1, tm),
            pl.BlockSpec((1, D_MODEL, tn), lambda i, j: (layer, 0, j)),
            pl.BlockSpec((1, D_MODEL, LR_PAD), lambda i, j: (layer, 0, 0)),
        ],
        out_specs=(pl.BlockSpec((tm, tn), lambda i, j: (i, j)),
                   pl.BlockSpec((tm, LR_PAD), lambda i, j: (i, 0))),
        scratch_shapes=[pltpu.VMEM((tm, D_MODEL), BF16)],
        compiler_params=_params(("arbitrary", "arbitrary")),
        name="in_proj_" + path.name,
    )(x, mods, mods, w_main, w_lr)


def _sink_softmax_pv(sinks, scores, values):
    heads = range(len(sinks))
    ms = [functools.reduce(jnp.maximum, [jnp.max(s, axis=-1, keepdims=True) for s in scores[h]],
                           sinks[h]) for h in heads]
    es = [[jnp.exp(s - ms[h]) for s in scores[h]] for h in heads]
    dens = [functools.reduce(lambda a, e: a + jnp.sum(e, axis=-1, keepdims=True), es[h],
                             jnp.exp(sinks[h] - ms[h])) for h in heads]
    invs = [1.0 / d for d in dens]
    ps = [[(e * invs[h]).astype(BF16) for e in es[h]] for h in heads]
    return [functools.reduce(lambda a, b: a + b, [_dot(p, v) for p, v in zip(ps[h], values[h])])
            for h in heads]


def _attn_ctx_kernel(sink_ref, q_ref, k_ref, v_ref, o_ref, *, layer):
    scale = HEAD_DIM ** -0.5
    heads = range(ATTN_HEADS)
    ks = [k_ref[:, kh * HEAD_DIM:(kh + 1) * HEAD_DIM].astype(BF16) for kh in range(ATTN_KV_HEADS)]
    vs = [v_ref[:, kh * HEAD_DIM:(kh + 1) * HEAD_DIM].astype(BF16) for kh in range(ATTN_KV_HEADS)]
    scores = [[_dot_nt(q_ref[:, h * HEAD_DIM:(h + 1) * HEAD_DIM].astype(BF16),
                       ks[h // ATTN_GROUP]) * scale] for h in heads]
    outs = _sink_softmax_pv([sink_ref[layer, h] for h in heads], scores,
                            [[vs[h // ATTN_GROUP]] for h in heads])
    for h in heads:
        o_ref[:, h * HEAD_DIM:(h + 1) * HEAD_DIM] = outs[h].astype(BF16)


def _attn_ctx(proj, sink, layer):
    return pl.pallas_call(
        functools.partial(_attn_ctx_kernel, layer=layer),
        out_shape=jax.ShapeDtypeStruct((N_CTX, ATTN_WIDTH), BF16),
        grid=(BATCH,),
        in_specs=[
            pl.BlockSpec(memory_space=pltpu.SMEM),
            pl.BlockSpec((SEQ, ATTN_WIDTH), lambda b: (b, 0)),
            pl.BlockSpec((SEQ, KV_WIDTH), lambda b: (b, COL_K // KV_WIDTH)),
            pl.BlockSpec((SEQ, KV_WIDTH), lambda b: (b, COL_V // KV_WIDTH)),
        ],
        out_specs=pl.BlockSpec((SEQ, ATTN_WIDTH), lambda b: (b, 0)),
        compiler_params=_params(("arbitrary",)),
        name="attn_ctx",
    )(sink, proj, proj, proj)


def _attn_lat_kernel(sink_ref, q_ref, k_ref, v_ref, ck_ref, cv_ref, cos_ref, sin_ref, o_ref, *,
                     layer):
    scale = HEAD_DIM ** -0.5
    win = 3 * BLOCK
    lane = lax.broadcasted_iota(jnp.int32, (1, HEAD_DIM), 1)
    low = (lane % (HEAD_DIM // 2)) < (HEAD_DIM // 4)

    def rope(x, c, s):
        partner = jnp.where(low, pltpu.roll(x, HEAD_DIM - HEAD_DIM // 4, 1),
                            pltpu.roll(x, HEAD_DIM // 4, 1))
        return x * c + partner * s

    kv_cols = [slice(kh * HEAD_DIM, (kh + 1) * HEAD_DIM) for kh in range(ATTN_KV_HEADS)]
    cks = [ck_ref[0, 0, :, c].astype(BF16) for c in kv_cols]
    cvs = [cv_ref[0, 0, :, c].astype(BF16) for c in kv_cols]
    sinks, scores, values, dests = [], [], [], []
    for sb in range(LAT_QBLOCKS):
        blk = pl.program_id(1) * LAT_QBLOCKS + sb
        q0 = pl.multiple_of(blk * BLOCK, BLOCK)
        start = pl.multiple_of(jnp.clip((blk - 1) * BLOCK, 0, DEC_SEQ - win), BLOCK)
        cos_q, sin_q = cos_ref[pl.ds(q0, BLOCK), :], sin_ref[pl.ds(q0, BLOCK), :]
        cos_k, sin_k = cos_ref[pl.ds(start, win), :], sin_ref[pl.ds(start, win), :]
        qpos = q0 + lax.broadcasted_iota(jnp.int32, (BLOCK, 1), 0)
        kpos = start + lax.broadcasted_iota(jnp.int32, (1, win), 1)
        valid = jnp.abs(qpos - kpos) <= WINDOW
        kws = [rope(k_ref[pl.ds(start, win), c], cos_k, sin_k).astype(BF16) for c in kv_cols]
        vws = [v_ref[pl.ds(start, win), c].astype(BF16) for c in kv_cols]
        rows = slice(sb * BLOCK, (sb + 1) * BLOCK)
        for h in range(ATTN_HEADS):
            kh = h // ATTN_GROUP
            q = rope(q_ref[rows, h * HEAD_DIM:(h + 1) * HEAD_DIM], cos_q, sin_q).astype(BF16)
            scores.append([jnp.where(valid, _dot_nt(q, kws[kh]) * scale, NEG_INF),
                           _dot_nt(q, cks[kh]) * scale])
            values.append([vws[kh], cvs[kh]])
            sinks.append(sink_ref[layer, h])
            dests.append((rows, h))
    outs = _sink_softmax_pv(sinks, scores, values)
    for (rows, h), o in zip(dests, outs):
        o_ref[rows, h * HEAD_DIM:(h + 1) * HEAD_DIM] = o.astype(BF16)


def _attn_lat(proj, sink, cache_k, cache_v, cos_t, sin_t, layer):
    nb = DEC_SEQ // (BLOCK * LAT_QBLOCKS)
    qrows = BLOCK * LAT_QBLOCKS
    cache_spec = pl.BlockSpec((1, 1, PAST_LEN, KV_WIDTH), lambda b, i: (b, layer, 0, 0))
    table_spec = pl.BlockSpec((DEC_SEQ, HEAD_DIM), lambda b, i: (0, 0))
    return pl.pallas_call(
        functools.partial(_attn_lat_kernel, layer=layer),
        out_shape=jax.ShapeDtypeStruct((N_LAT, ATTN_WIDTH), BF16),
        grid=(DEC_BATCH, nb),
        in_specs=[
            pl.BlockSpec(memory_space=pltpu.SMEM),
            pl.BlockSpec((qrows, ATTN_WIDTH), lambda b, i: (b * nb + i, 0)),
            pl.BlockSpec((DEC_SEQ, KV_WIDTH), lambda b, i: (b, COL_K // KV_WIDTH)),
            pl.BlockSpec((DEC_SEQ, KV_WIDTH), lambda b, i: (b, COL_V // KV_WIDTH)),
            cache_spec, cache_spec, table_spec, table_spec,
        ],
        out_specs=pl.BlockSpec((qrows, ATTN_WIDTH), lambda b, i: (b * nb + i, 0)),
        compiler_params=_params(("arbitrary", "arbitrary")),
        name="attn_lat",
    )(sink, proj, proj, proj,
      cache_k.reshape(DEC_BATCH, DEPTH, PAST_LEN, KV_WIDTH),
      cache_v.reshape(DEC_BATCH, DEPTH, PAST_LEN, KV_WIDTH), cos_t, sin_t)


def _rope_tables():
    half = HEAD_DIM // 2
    n_freq = half // 2
    t = jnp.arange(DEC_SEQ)
    row = (t // GRID_W).astype(F32)
    col = (t % GRID_W).astype(F32)
    freqs = ROPE_THETA ** (-jnp.arange(n_freq, dtype=F32) / n_freq)
    tabs = []
    for pos in (row, col):
        ang = pos[:, None] * freqs[None, :]
        tabs.append((jnp.cos(ang), jnp.sin(ang)))
    cos_t = jnp.concatenate([tabs[0][0], tabs[0][0], tabs[1][0], tabs[1][0]], axis=-1)
    sin_t = jnp.concatenate([-tabs[0][1], tabs[0][1], -tabs[1][1], tabs[1][1]], axis=-1)
    return cos_t, sin_t


def _log_sigmoid(x):
    return jnp.minimum(x, 0.0) - jnp.log(1.0 + jnp.exp(-jnp.abs(x)))


def _gla_kernel(*refs, seq, n_heads, v_split, is_ctx, gpi):
    q_ref, k_ref = refs[:2]
    v_refs = refs[2:2 + v_split]
    og_refs = refs[2 + v_split:2 + 2 * v_split]
    n_in = 8 + 2 * v_split
    lr_ref, wgf_ref, wgb_ref, bgf_ref, bgb_ref, nw_ref = refs[2 + 2 * v_split:n_in]
    if is_ctx:
        o_ref, sf_ref, sb_ref = refs[-16:-13]
    else:
        s0f_ref, s0b_ref = refs[n_in:n_in + 2]
        o_ref = refs[-14]
    heads_per_block = n_heads // v_split

    def head_cols(block_refs, h):
        hb = h % heads_per_block
        return block_refs[h // heads_per_block], slice(hb * GLA_DV, (hb + 1) * GLA_DV)
    (gf_scr, gb_scr, qef_scr, qeb_scr, o_scr, uf_scr, ub_scr, sinf_scr, sinb_scr,
     decf_scr, decb_scr, stf_scr, stb_scr) = refs[-13:]
    c = GLA_CHUNK
    grp = GLA_GROUP
    cpg = grp // c
    n_chunks = seq // c
    n_groups = seq // grp
    q_scale = GLA_DK ** -0.5

    lr = lr_ref[...].astype(BF16)
    gf_scr[...] = _log_sigmoid(_dot(lr, wgf_ref[0]) + bgf_ref[0]) / GATE_NORM
    gb_scr[...] = _log_sigmoid(_dot(lr, wgb_ref[0]) + bgb_ref[0]) / GATE_NORM

    ri = lax.broadcasted_iota(jnp.int32, (grp, grp), 0)
    ci = lax.broadcasted_iota(jnp.int32, (grp, grp), 1)
    same_chunk = (ri // c) == (ci // c)
    keep_f = same_chunk & (ri >= ci)
    keep_b = same_chunk & (ri <= ci)

    pos = lax.broadcasted_iota(jnp.int32, (grp, 1), 0) & (c - 1)

    def cumsum(forward, g):
        y = g
        s = 1
        while s < c:
            if forward:
                y = y + jnp.where(pos >= s, pltpu.roll(y, s, 0), 0.0)
            else:
                y = y + jnp.where(pos < c - s, pltpu.roll(y, grp - s, 0), 0.0)
            s *= 2
        return y

    dirs = ((gf_scr, keep_f, c - 1, qef_scr, uf_scr, decf_scr),
            (gb_scr, keep_b, 0, qeb_scr, ub_scr, decb_scr))

    hks = [slice(h * GLA_DK, (h + 1) * GLA_DK) for h in range(n_heads)]
    hvs = [slice(h * GLA_DV, (h + 1) * GLA_DV) for h in range(n_heads)]

    def group_body(it, carry):
        items = [(it * gpi + gi, h) for gi in range(gpi) for h in range(n_heads)]
        rows = [pl.ds(pl.multiple_of(r * grp, grp), grp) for r, _ in items]
        chains = [(x, d) for x in range(len(items)) for d in range(2)]
        qs = [q_ref[rows[x], hks[h]] * q_scale for x, (_, h) in enumerate(items)]
        ks = [k_ref[rows[x], hks[h]] for x, (_, h) in enumerate(items)]
        vs = [head_cols(v_refs, h)[0][rows[x], head_cols(v_refs, h)[1]].astype(BF16)
              for x, (_, h) in enumerate(items)]
        bs = [cumsum(d == 0, dirs[d][0][rows[x], hks[items[x][1]]]) for x, d in chains]
        ebs = [jnp.exp(b) for b in bs]
        embs = [jnp.exp(-b) for b in bs]
        qes = [(qs[x] * ebs[i]).astype(BF16) for i, (x, d) in enumerate(chains)]
        kes = [ks[x] * embs[i] for i, (x, d) in enumerate(chains)]
        for i, (x, d) in enumerate(chains):
            dirs[d][3][rows[x], hks[items[x][1]]] = qes[i]
        avals = [jnp.where(dirs[d][1], _dot_nt(qes[i], kes[i].astype(BF16)), 0.0)
                 for i, (x, d) in enumerate(chains)]
        decays = [[jnp.exp(bs[i][cc * c + dirs[d][2]:cc * c + dirs[d][2] + 1, :])
                   for cc in range(cpg)] for i, (x, d) in enumerate(chains)]
        kds = [[(kes[i][cc * c:(cc + 1) * c] * decays[i][cc]).astype(BF16) for cc in range(cpg)]
               for i in range(len(chains))]
        for i, (x, d) in enumerate(chains):
            r, h = items[x]
            for cc in range(cpg):
                n = r * cpg + cc
                dirs[d][4][h, n] = _dot_tn(vs[x][cc * c:(cc + 1) * c], kds[i][cc])
                dirs[d][5][h, n] = jnp.broadcast_to(decays[i][cc], (8, GLA_DK))
        for x, (_, h) in enumerate(items):
            a_sum = (avals[2 * x] + avals[2 * x + 1]).astype(BF16)
            o_scr[rows[x], hvs[h]] = _dot(a_sum, vs[x])
        return carry

    lax.fori_loop(0, n_groups // gpi, group_body, 0)

    for h in range(n_heads):
        if is_ctx:
            stf_scr[h] = jnp.zeros((GLA_DV, GLA_DK), F32)
            stb_scr[h] = jnp.zeros((GLA_DV, GLA_DK), F32)
        else:
            stf_scr[h] = s0f_ref[0, 0, h].T
            stb_scr[h] = s0b_ref[0, 0, h].T

    def scan_body(n, carry):
        nb = n_chunks - 1 - n
        for h in range(n_heads):
            s = stf_scr[h]
            sinf_scr[h, n] = s.astype(BF16)
            stf_scr[h] = decf_scr[h, n][0:1, :] * s + uf_scr[h, n]
            s = stb_scr[h]
            sinb_scr[h, nb] = s.astype(BF16)
            stb_scr[h] = decb_scr[h, nb][0:1, :] * s + ub_scr[h, nb]
        return carry

    lax.fori_loop(0, n_chunks, scan_body, 0)
    if is_ctx:
        for h in range(n_heads):
            sf_ref[0, 0, h] = stf_scr[h].T
            sb_ref[0, 0, h] = stb_scr[h].T

    nw = nw_ref[0]

    def finish(it, carry):
        items = [(it * gpi + gi, h) for gi in range(gpi) for h in range(n_heads)]
        rows = [pl.ds(pl.multiple_of(r * grp, grp), grp) for r, _ in items]
        inter = []
        for r, h in items:
            parts = []
            for cc in range(cpg):
                n = r * cpg + cc
                cr = pl.ds(pl.multiple_of(r * grp + cc * c, c), c)
                qe2 = jnp.concatenate([qef_scr[cr, hks[h]], qeb_scr[cr, hks[h]]], axis=1)
                s2 = jnp.concatenate([sinf_scr[h, n], sinb_scr[h, n]], axis=1)
                parts.append(_dot_nt(qe2, s2))
            inter.append(jnp.concatenate(parts, axis=0))
        os = [o_scr[rows[x], hvs[h]] + inter[x] for x, (_, h) in enumerate(items)]
        scales = [lax.rsqrt(jnp.mean(o * o, axis=-1, keepdims=True) + LN_EPS) for o in os]
        gates = [_silu(head_cols(og_refs, h)[0][rows[x], head_cols(og_refs, h)[1]])
                 for x, (_, h) in enumerate(items)]
        for x, (_, h) in enumerate(items):
            o_ref[rows[x], hvs[h]] = (os[x] * scales[x] * nw * gates[x]).astype(BF16)
        return carry

    lax.fori_loop(0, n_groups // gpi, finish, 0)


def _gla(path, proj, proj_lr, wgf, wgb, bgf, bgb, norm_w, layer, *, s0=None, states=None):
    seq = path.seq
    n_heads = GLA_HEADS if path.is_ctx else 1
    n_chunks = seq // GLA_CHUNK
    n_groups = seq // GLA_GROUP

    v_split = 2 if n_heads == GLA_HEADS else 1

    def col_spec(width, col, split=1, part=0):
        w = n_heads * width // split
        assert col % w == 0
        return pl.BlockSpec((seq, w), lambda b, h: (b, col // w + h * split + part))

    head_w = pl.BlockSpec((1, LR_PAD, n_heads * GLA_DK), lambda b, h: (layer, 0, h))
    head_b = pl.BlockSpec((1, 1, n_heads * GLA_DK), lambda b, h: (layer, 0, h))
    state_spec = pl.BlockSpec((1, 1, n_heads, GLA_DK, GLA_DV), lambda b, h: (b, layer, h, 0, 0))
    in_specs = [
        col_spec(GLA_DK, COL_GQ), col_spec(GLA_DK, COL_GK),
        *[col_spec(GLA_DV, COL_GV, v_split, p) for p in range(v_split)],
        *[col_spec(GLA_DV, COL_GOG, v_split, p) for p in range(v_split)],
        pl.BlockSpec((seq, LR_PAD), lambda b, h: (b, 0)),
        head_w, head_w, head_b, head_b,
        pl.BlockSpec((1, 1, GLA_DV), lambda b, h: (layer, 0, 0)),
    ]
    args = [proj] * (2 + 2 * v_split) + [proj_lr, wgf, wgb, bgf, bgb, norm_w]
    o_shape = jax.ShapeDtypeStruct((path.rows, GLA_WIDTH), BF16)
    o_spec = pl.BlockSpec((seq, n_heads * GLA_DV), lambda b, h: (b, h))
    aliases = {}
    if path.is_ctx:
        state_shape = jax.ShapeDtypeStruct((BATCH, DEPTH, GLA_HEADS, GLA_DK, GLA_DV), F32)
        out_shape = (o_shape, state_shape, state_shape)
        out_specs = (o_spec, state_spec, state_spec)
        in_specs += [pl.BlockSpec(memory_space=pl.ANY), pl.BlockSpec(memory_space=pl.ANY)]
        args += list(states)
        aliases = {len(args) - 2: 1, len(args) - 1: 2}
    else:
        in_specs += [state_spec, state_spec]
        args += list(s0)
        out_shape = (o_shape,)
        out_specs = (o_spec,)
    per_chunk = (n_heads, n_chunks, GLA_DV, GLA_DK)
    return pl.pallas_call(
        functools.partial(_gla_kernel, seq=seq, n_heads=n_heads, v_split=v_split, is_ctx=path.is_ctx,
                          gpi=min(n_groups, GLA_CHAINS // (2 * n_heads))),
        out_shape=out_shape,
        grid=(path.n_batch, GLA_HEADS // n_heads),
        in_specs=in_specs,
        out_specs=out_specs,
        input_output_aliases=aliases,
        scratch_shapes=[
            pltpu.VMEM((seq, n_heads * GLA_DK), F32), pltpu.VMEM((seq, n_heads * GLA_DK), F32),
            pltpu.VMEM((seq, n_heads * GLA_DK), BF16), pltpu.VMEM((seq, n_heads * GLA_DK), BF16),
            pltpu.VMEM((seq, n_heads * GLA_DV), F32),
            pltpu.VMEM(per_chunk, F32), pltpu.VMEM(per_chunk, F32),
            pltpu.VMEM(per_chunk, BF16), pltpu.VMEM(per_chunk, BF16),
            pltpu.VMEM((n_heads, n_chunks, 8, GLA_DK), F32),
            pltpu.VMEM((n_heads, n_chunks, 8, GLA_DK), F32),
            pltpu.VMEM((n_heads, GLA_DV, GLA_DK), F32), pltpu.VMEM((n_heads, GLA_DV, GLA_DK), F32),
        ],
        compiler_params=_params(("arbitrary", "arbitrary")),
        name="gla_" + path.name,
    )(*args)


def _outproj_kernel(oa_ref, og_ref, x_ref, g_ref, w_ref, lw_ref, lb_ref, o_ref):
    rb = OUT_ROW_BLOCK
    blocks = [slice(r0, r0 + rb) for r0 in range(0, o_ref.shape[0], rb)]

    def matmul(r):
        return (_dot(oa_ref[r, :], w_ref[0, :ATTN_WIDTH, :])
                + _dot(og_ref[r, :], w_ref[0, ATTN_WIDTH:, :]))

    def norm(r, f):
        y = DEEPNORM_ALPHA * x_ref[r, :] + g_ref[0] * f
        o_ref[r, :] = _layer_norm(y, lw_ref[0], lb_ref[0])

    f_prev = matmul(blocks[0])
    for k in range(1, len(blocks)):
        f_next = matmul(blocks[k])
        norm(blocks[k - 1], f_prev)
        f_prev = f_next
    norm(blocks[-1], f_prev)


def _out_proj(path, o_att, o_gla, x, mods, w_out, ln_w, ln_b, layer):
    tm = TM_OUT
    vec = _layer_vec_spec(layer, D_MODEL, 1)
    return pl.pallas_call(
        _outproj_kernel,
        out_shape=jax.ShapeDtypeStruct((path.rows, D_MODEL), F32),
        grid=(path.rows // tm,),
        in_specs=[
            pl.BlockSpec((tm, ATTN_WIDTH), lambda i: (i, 0)),
            pl.BlockSpec((tm, GLA_WIDTH), lambda i: (i, 0)),
            pl.BlockSpec((tm, D_MODEL), lambda i: (i, 0)),
            _mod_spec(path, layer, 2, tm),
            pl.BlockSpec((1, D_MODEL, D_MODEL), lambda i: (layer, 0, 0)),
            vec, vec,
        ],
        out_specs=pl.BlockSpec((tm, D_MODEL), lambda i: (i, 0)),
        compiler_params=_params(("arbitrary",)),
        name="out_proj_" + path.name,
    )(o_att, o_gla, x, mods, w_out, ln_w, ln_b)


def _ffn_kernel(*refs, tm, nf, seq, has_halo):
    if has_halo:
        x_ref, xp_ref, xn_ref = refs[:3]
        refs = refs[3:]
    else:
        x_ref = refs[0]
        refs = refs[1:]
    (sh_ref, sc_ref, g_ref, wa_ref, wg_ref, cp_ref, wd_ref, lw_ref,
     lb_ref, o_ref, h_scr, ua0_scr, ug0_scr, ua1_scr, ug1_scr, act0_scr, act1_scr) = refs
    i = pl.program_id(0)
    j = pl.program_id(1)
    u_slots = ((ua0_scr, ug0_scr), (ua1_scr, ug1_scr))
    act_slots = (act0_scr, act1_scr)
    tf = wa_ref.shape[-1]
    hrows = HALO if has_halo else 0
    h_rows = tm + 2 * hrows
    seg = tm if has_halo else seq

    def u_row(r):
        return r if has_halo else GAP + r + (r // seg) * GAP

    up_n, up_m = tf // 2, h_rows // FFN_M_SPLIT
    act_r, act_c = 128, 128
    down_m, down_n = tm // FFN_M_SPLIT, D_MODEL // 4

    def up_piece(slot, which, n0, m0):
        w_ref = (wa_ref, wg_ref)[which]
        u = _dot(h_scr[m0:m0 + up_m, :], w_ref[0, :, n0:n0 + up_n])
        run_len = up_m if has_halo else seg
        for r in range(0, up_m, run_len):
            s0 = u_row(m0 + r)
            u_slots[slot][which][s0:s0 + run_len, n0:n0 + up_n] = u[r:r + run_len]

    def down_piece(slot, m0, n0):
        o_ref[m0:m0 + down_m, n0:n0 + down_n] += _dot(act_slots[slot][m0:m0 + down_m, :],
                                                      wd_ref[0, :, n0:n0 + down_n])

    def act_piece(slot, r0, c0):
        cols = slice(c0, c0 + act_c)
        s0 = u_row(r0 + hrows)

        def conv(u_ref, which):
            cp = cp_ref[0, 0, :, which * tf + c0:which * tf + c0 + act_c]
            return (u_ref[s0 - 1:s0 - 1 + act_r, cols] * cp[0:1] + u_ref[s0:s0 + act_r, cols] * cp[1:2]
                    + u_ref[s0 + 1:s0 + 1 + act_r, cols] * cp[2:3] + cp[3:4])

        a = conv(u_slots[slot][0], 0)
        g = conv(u_slots[slot][1], 1)
        act_slots[slot][r0:r0 + act_r, cols] = (_silu(g) * a).astype(BF16)

    def run(up=None, act=None, down=None):
        pieces = []

        def add(fn, arg_list):
            for k, a in enumerate(arg_list):
                pieces.append(((k + 0.5) / len(arg_list), len(pieces), fn, a))

        if up is not None:
            add(up_piece, [(up, w, n0, m0) for w in (0, 1) for n0 in range(0, tf, up_n)
                           for m0 in range(0, h_rows, up_m)])
        if act is not None:
            add(act_piece, [(act, r0, c0) for c0 in range(0, tf, act_c)
                            for r0 in range(0, tm, act_r)])
        if down is not None:
            add(down_piece, [(down, m0, n0) for n0 in range(0, D_MODEL, down_n)
                             for m0 in range(0, tm, down_m)])
        for _, _, fn, a in sorted(pieces, key=lambda t: t[:2]):
            fn(*a)

    @pl.when(j == 0)
    def _():
        sc = 1.0 + sc_ref[0]
        sh = sh_ref[0]

        def build_h(part):
            xr = slice(part * (tm // 2), (part + 1) * (tm // 2))
            h_scr[hrows + xr.start:hrows + xr.stop, :] = (x_ref[xr, :] * sc + sh).astype(BF16)
            if has_halo:
                tile_in_seq = i % (seq // tm)
                if part == 0:
                    h_scr[0:HALO, :] = jnp.where(tile_in_seq != 0, xp_ref[...] * sc + sh,
                                                 0.0).astype(BF16)
                else:
                    h_scr[HALO + tm:2 * HALO + tm, :] = jnp.where(
                        tile_in_seq != seq // tm - 1, xn_ref[...] * sc + sh, 0.0).astype(BF16)

        if not has_halo:
            for pair in u_slots:
                for u_scr in pair:
                    for r in range(0, tm + 1, seg):
                        u_scr[u_row(r) - GAP:u_row(r), :] = jnp.zeros((GAP, tf), F32)
        build_h(0)
        if FFN_M_SPLIT == 1:
            build_h(1)
        pieces = [(0, w, n0, m0) for m0 in range(0, h_rows, up_m) for w in (0, 1)
                  for n0 in range(0, tf, up_n)]
        zr = tm // len(pieces)
        for k, a in enumerate(pieces):
            up_piece(*a)
            if k == 0 and FFN_M_SPLIT == 2:
                build_h(1)
            o_ref[k * zr:(k + 1) * zr, :] = jnp.zeros((zr, D_MODEL), F32)

    @pl.when(j == 1)
    def _():
        run(up=1, act=0)

    for parity in (0, 1):
        @pl.when((j >= 2) & (j < nf) & (j % 2 == parity))
        def _(parity=parity):
            run(up=parity, act=1 - parity, down=parity)

    @pl.when(j == nf)
    def _():
        run(act=(nf - 1) % 2, down=nf % 2)

    @pl.when(j == nf + 1)
    def _():
        for m0 in range(0, tm, down_m):
            for n0 in range(0, D_MODEL, down_n):
                down_piece((nf - 1) % 2, m0, n0)
        for m0 in range(0, tm, down_m):
            rows = slice(m0, m0 + down_m)
            y = DEEPNORM_ALPHA * x_ref[rows, :] + g_ref[0] * o_ref[rows, :]
            o_ref[rows, :] = _layer_norm(y, lw_ref[0], lb_ref[0])


def _pack_conv_params(conv_w, conv_b):
    nf = D_FF // TF_FFN
    cw = conv_w.reshape(DEPTH, 3, 2, nf, TF_FFN).transpose(0, 3, 1, 2, 4).reshape(DEPTH, nf, 3, 2 * TF_FFN)
    cb = conv_b.reshape(DEPTH, 1, 2, nf, TF_FFN).transpose(0, 3, 1, 2, 4).reshape(DEPTH, nf, 1, 2 * TF_FFN)
    return jnp.concatenate([cw, cb, jnp.zeros((DEPTH, nf, 4, 2 * TF_FFN), F32)], axis=2)


def _ffn(path, x, mods, w_up, conv_p, w_down, ln_w, ln_b, layer):
    tm, tf = TM_FFN, TF_FFN
    nf = D_FF // tf
    hb = tm // HALO
    n_halo_blocks = path.rows // HALO
    has_halo = path.seq > tm
    vec = _layer_vec_spec(layer, D_MODEL, 2)

    def chunk(j, lag):
        return jnp.clip(j - lag, 0, nf - 1)

    x_specs = [pl.BlockSpec((tm, D_MODEL), lambda i, j: (i, 0))]
    x_args = [x]
    if has_halo:
        x_specs += [
            pl.BlockSpec((HALO, D_MODEL), lambda i, j: (jnp.maximum(i * hb - 1, 0), 0)),
            pl.BlockSpec((HALO, D_MODEL),
                         lambda i, j: (jnp.minimum((i + 1) * hb, n_halo_blocks - 1), 0)),
        ]
        x_args += [x, x]
    h_rows = tm + 2 * HALO if has_halo else tm
    u_rows = tm + 2 * HALO if has_halo else GAP + (tm // path.seq) * (path.seq + GAP)
    u_scratch = pltpu.VMEM((u_rows, tf), F32)
    act_scratch = pltpu.VMEM((tm, tf), BF16)
    return pl.pallas_call(
        functools.partial(_ffn_kernel, tm=tm, nf=nf, seq=path.seq, has_halo=has_halo),
        out_shape=jax.ShapeDtypeStruct((path.rows, D_MODEL), F32),
        grid=(path.rows // tm, nf + 2),
        in_specs=x_specs + [
            _mod_spec(path, layer, 3, tm), _mod_spec(path, layer, 4, tm),
            _mod_spec(path, layer, 5, tm),
            pl.BlockSpec((1, D_MODEL, tf), lambda i, j: (layer, 0, chunk(j, 0))),
            pl.BlockSpec((1, D_MODEL, tf), lambda i, j: (layer, 0, nf + chunk(j, 0))),
            pl.BlockSpec((1, 1, 8, 2 * tf), lambda i, j: (layer, chunk(j, 1), 0, 0)),
            pl.BlockSpec((1, tf, D_MODEL), lambda i, j: (layer, chunk(j, 2), 0)),
            vec, vec,
        ],
        out_specs=pl.BlockSpec((tm, D_MODEL), lambda i, j: (i, 0), pipeline_mode=pl.Buffered(1)),
        scratch_shapes=[pltpu.VMEM((h_rows, D_MODEL), BF16),
                        u_scratch, u_scratch, u_scratch, u_scratch, act_scratch, act_scratch],
        compiler_params=_params(("arbitrary", "arbitrary")),
        name="ffn_" + path.name,
    )(*x_args, mods, mods, mods, w_up, w_up, conv_p, w_down, ln_w, ln_b)


def _pad_gate(w, row0):
    return jnp.pad(w, ((0, 0), (row0, LR_PAD - GATE_RANK - row0), (0, 0))).astype(BF16)


def kernel(x_prompt, x_sample, cache_k, cache_v, state_gla_fwd, state_gla_bwd, c, c_ctx, w_ada,
           b_ada, w_in, attn_sink, w_gate_f, b_gate_f, w_gate_b, b_gate_b, gla_norm_w, w_out,
           ln1_w, ln1_b, w_up, conv_w, conv_b, w_down, ln2_w, ln2_b):
    cc = jnp.concatenate([c_ctx[None, :], c, jnp.zeros((MOD_ROWS - 1 - DEC_BATCH, D_MODEL), F32)], axis=0)
    mods = _ada(cc, w_ada, b_ada).reshape(DEPTH * MOD_ROWS * N_MODS, 1, D_MODEL)
    cos_t, sin_t = _rope_tables()

    w_main = w_in.astype(BF16)
    w_lr = jnp.pad(w_in[:, :, PROJ_MAIN:], ((0, 0), (0, 0), (0, LR_PAD - 2 * GATE_RANK))).astype(BF16)
    w_up_b = w_up.astype(BF16)
    w_down_b = w_down.astype(BF16)
    w_out_b = w_out.astype(BF16)
    wgf = _pad_gate(w_gate_f, 0)
    wgb = _pad_gate(w_gate_b, GATE_RANK)
    bgf = b_gate_f[:, None, :]
    bgb = b_gate_b[:, None, :]
    norm_w = gla_norm_w[:, None, :]
    conv_p = _pack_conv_params(conv_w, conv_b)
    ln1_w3, ln1_b3, ln2_w3, ln2_b3 = (a[:, None, :] for a in (ln1_w, ln1_b, ln2_w, ln2_b))

    xs = {CTX: x_prompt.reshape(N_CTX, D_MODEL), LAT: x_sample.reshape(N_LAT, D_MODEL)}
    new_k, new_v = [], []
    states = (jnp.zeros((BATCH, DEPTH, GLA_HEADS, GLA_DK, GLA_DV), F32),) * 2
    for l in range(DEPTH):
        for path in (CTX, LAT):
            x = xs[path]
            proj, proj_lr = _in_proj(path, x, mods, w_main, w_lr, l)
            if path.is_ctx:
                o_att = _attn_ctx(proj, attn_sink, l)
                o_gla, sf, sb = _gla(path, proj, proj_lr, wgf, wgb, bgf, bgb, norm_w, l, states=states)
                states = (sf, sb)
                new_k.append(proj[:, COL_K:COL_K + KV_WIDTH].reshape(BATCH, SEQ, ATTN_KV_HEADS, HEAD_DIM))
                new_v.append(proj[:, COL_V:COL_V + KV_WIDTH].reshape(BATCH, SEQ, ATTN_KV_HEADS, HEAD_DIM))
            else:
                o_att = _attn_lat(proj, attn_sink, cache_k, cache_v, cos_t, sin_t, l)
                (o_gla,) = _gla(path, proj, proj_lr, wgf, wgb, bgf, bgb, norm_w, l,
                                s0=(state_gla_fwd, state_gla_bwd))
            x = _out_proj(path, o_att, o_gla, x, mods, w_out_b, ln1_w3, ln1_b3, l)
            xs[path] = _ffn(path, x, mods, w_up_b, conv_p, w_down_b, ln2_w3, ln2_b3, l)

    return (xs[CTX].reshape(BATCH, SEQ, D_MODEL), xs[LAT].reshape(DEC_BATCH, DEC_SEQ, D_MODEL),
            jnp.stack(new_k, axis=1), jnp.stack(new_v, axis=1), states[0], states[1])
```
